```python
import math
import jax, jax.numpy as jnp
from jax import lax
import numpy as np

D_MODEL = 2048
BATCH = 2
SEQ = 4096
DEPTH = 2
DEC_BATCH = 128
DEC_SEQ = 8
PAST_LEN = 2048
PAGE_SIZE = 128

N_MEM = 256
MEM_HEADS = 4
MEM_HEAD_DIM = D_MODEL // MEM_HEADS
D_MEM = MEM_HEADS * MEM_HEAD_DIM
SSD_EXPAND = 2
D_INNER = SSD_EXPAND * D_MODEL
SSD_HEAD_DIM = 64
SSD_HEADS = D_INNER // SSD_HEAD_DIM
SSD_GROUPS = 8
SSD_HPG = SSD_HEADS // SSD_GROUPS
SSD_STATE = 128
SSD_CONV = 4
SSD_CHUNK = 256
CONV_DIM = D_INNER + 2 * SSD_GROUPS * SSD_STATE
SB_HEAD_DIM = 128
SB_HEADS = D_MODEL // SB_HEAD_DIM
D_SB = SB_HEADS * SB_HEAD_DIM
SB_BIAS_INIT = -8.0
Q_BLOCK = 128
D_FF = ((8 * D_MODEL // 3 + 255) // 256) * 256
FFN_RES = 0.5
N_SSD_LAYERS = (DEPTH + 1) // 2
N_SB_LAYERS = DEPTH // 2
EPS = 1e-6

kernel_name = 'hybrid_ssd_stickbreak_memxattn_macaron_step'


def _rmsnorm(x, g):
    xf = x.astype(jnp.float32)
    y = xf * lax.rsqrt(jnp.mean(xf * xf, axis=-1, keepdims=True) + EPS)
    return (y * g.astype(jnp.float32)).astype(x.dtype)


def _swiglu(h, w_in, w_out):
    gate, up = jnp.split(h @ w_in, 2, axis=-1)
    return (jax.nn.silu(gate) * up) @ w_out


def _mem_kv(mem, g, w_kv):
    k, v = jnp.split(_rmsnorm(mem, g) @ w_kv, 2, axis=-1)
    shp = mem.shape[:2] + (MEM_HEADS, MEM_HEAD_DIM)
    return k.reshape(shp), v.reshape(shp)


def _mem_attend(q, mk, mv):
    b, t, _ = q.shape
    qh = q.reshape(b, t, MEM_HEADS, MEM_HEAD_DIM)
    s = jnp.einsum('bthd,bmhd->bhtm', qh, mk.astype(q.dtype), preferred_element_type=jnp.float32) * (MEM_HEAD_DIM ** -0.5)
    p = jax.nn.softmax(s, axis=-1)
    o = jnp.einsum('bhtm,bmhd->bthd', p.astype(q.dtype), mv.astype(q.dtype))
    return o.reshape(b, t, D_MEM)


def _causal_conv(xbc, buf, w, bias):
    xp = jnp.concatenate([buf.astype(xbc.dtype), xbc], axis=1)
    y = lax.conv_general_dilated(xp, w[:, None, :].astype(xbc.dtype), window_strides=(1,), padding='VALID',
                                 dimension_numbers=('NWC', 'WIO', 'NWC'), feature_group_count=CONV_DIM)
    return jax.nn.silu(y + bias.astype(xbc.dtype)), xp[:, -(SSD_CONV - 1):]


def _ssd_scan(x, dt, a, bm, cm, h0):
    b, L = x.shape[:2]
    q = math.gcd(L, SSD_CHUNK)
    c = L // q
    G, R, P, N = SSD_GROUPS, SSD_HPG, SSD_HEAD_DIM, SSD_STATE
    x = x.reshape(b, c, q, G, R, P)
    dt = dt.reshape(b, c, q, G, R)
    bm = bm.reshape(b, c, q, G, N)
    cm = cm.reshape(b, c, q, G, N)
    cs = jnp.cumsum(dt * a.reshape(G, R), axis=2)
    xdt = x * dt[..., None]
    cs_t = jnp.moveaxis(cs, 2, -1)
    tril = jnp.tril(jnp.ones((q, q), dtype=bool))
    decay = jnp.exp(jnp.where(tril, cs_t[..., :, None] - cs_t[..., None, :], -jnp.inf))
    cb = jnp.einsum('bctgn,bcsgn->bcgts', cm, bm)
    y_diag = jnp.einsum('bcgts,bcgrts,bcsgrp->bctgrp', cb, decay, xdt)
    decay_to_end = jnp.exp(cs[:, :, -1:] - cs)
    states = jnp.einsum('bcsgn,bcsgr,bcsgrp->bcgrpn', bm, decay_to_end, xdt)
    chunk_decay = jnp.exp(cs[:, :, -1])

    def step(h, inp):
        st, dec = inp
        return h * dec[..., None, None] + st, h

    h_fin, h_prev = lax.scan(step, h0.reshape(b, G, R, P, N),
                             (jnp.moveaxis(states, 1, 0), jnp.moveaxis(chunk_decay, 1, 0)))
    h_prev = jnp.moveaxis(h_prev, 0, 1)
    y_off = jnp.einsum('bctgn,bcgrpn,bctgr->bctgrp', cm, h_prev, jnp.exp(cs))
    return (y_diag + y_off).reshape(b, L, SSD_HEADS, P), h_fin.reshape(b, SSD_HEADS, P, N)


def _ssd_mixer(h, conv_buf, ssm0, mk, mv, w_in, conv_w, conv_b, dt_bias, a_log, d_skip, g_norm, w_out):
    b, L, _ = h.shape
    z, xbc, dt_raw, q_mem = jnp.split(h @ w_in, [D_INNER, D_INNER + CONV_DIM, D_INNER + CONV_DIM + SSD_HEADS], axis=-1)
    xbc, new_buf = _causal_conv(xbc, conv_buf, conv_w, conv_b)
    xs, bm, cm = jnp.split(xbc, [D_INNER, D_INNER + SSD_GROUPS * SSD_STATE], axis=-1)
    dt = jax.nn.softplus(dt_raw.astype(jnp.float32) + dt_bias.astype(jnp.float32))
    a = -jnp.exp(a_log.astype(jnp.float32))
    xh = xs.reshape(b, L, SSD_HEADS, SSD_HEAD_DIM).astype(jnp.float32)
    y, h_fin = _ssd_scan(xh, dt, a,
                         bm.reshape(b, L, SSD_GROUPS, SSD_STATE).astype(jnp.float32),
                         cm.reshape(b, L, SSD_GROUPS, SSD_STATE).astype(jnp.float32),
                         ssm0.astype(jnp.float32))
    y = y + d_skip.astype(jnp.float32)[:, None] * xh
    yg = (y.reshape(b, L, D_INNER) * jax.nn.silu(z.astype(jnp.float32))).reshape(b, L, SSD_GROUPS, D_INNER // SSD_GROUPS)
    yg = yg * lax.rsqrt(jnp.mean(yg * yg, axis=-1, keepdims=True) + EPS)
    yg = (yg.reshape(b, L, D_INNER) * g_norm.astype(jnp.float32)).astype(h.dtype)
    o_mem = _mem_attend(q_mem, mk, mv)
    out = jnp.concatenate([yg, o_mem], axis=-1) @ w_out
    return out, new_buf, h_fin.astype(ssm0.dtype)


def _sb_apply(q, k, v, bias, q_idx, k_idx):
    zz = jnp.einsum('bqhd,bkhd->bhqk', q, k, preferred_element_type=jnp.float32) * (SB_HEAD_DIM ** -0.5)
    zz = zz + bias.astype(jnp.float32)[None, :, None, None]
    mask = k_idx[None, :] < q_idx[:, None]
    log_keep = jnp.where(mask, jax.nn.log_sigmoid(-zz), 0.0)
    after = lax.cumsum(log_keep, axis=3, reverse=True) - log_keep
    attn = jnp.where(mask, jnp.exp(jax.nn.log_sigmoid(zz) + after), 0.0)
    return jnp.einsum('bhqk,bkhd->bqhd', attn.astype(v.dtype), v)


def _sb_project(h, w_in):
    b, L, _ = h.shape
    q, k, v, q_mem = jnp.split(h @ w_in, [D_SB, 2 * D_SB, 3 * D_SB], axis=-1)
    shp = (b, L, SB_HEADS, SB_HEAD_DIM)
    return q.reshape(shp), k.reshape(shp), v.reshape(shp), q_mem


def _sb_prompt(q, k, v, bias):
    b, L = q.shape[:2]
    nb = L // Q_BLOCK
    qb = jnp.moveaxis(q.reshape(b, nb, Q_BLOCK, SB_HEADS, SB_HEAD_DIM), 1, 0)
    k_idx = jnp.arange(L)

    def blk(args):
        qi, i = args
        return _sb_apply(qi, k, v, bias, i * Q_BLOCK + jnp.arange(Q_BLOCK), k_idx)

    o = lax.map(blk, (qb, jnp.arange(nb)))
    return jnp.moveaxis(o, 0, 1).reshape(b, L, D_SB)


def _sb_sample(q, k, v, bias, cache_k, cache_v, j, page_table):
    b, t = q.shape[:2]
    past = page_table.shape[1] * PAGE_SIZE
    q_idx = past + jnp.arange(t)
    k_idx = jnp.arange(past + t)

    def one(args):
        qi, ki, vi, pt = args
        kp = cache_k[j, pt].reshape(past, SB_HEADS, SB_HEAD_DIM).astype(ki.dtype)
        vp = cache_v[j, pt].reshape(past, SB_HEADS, SB_HEAD_DIM).astype(vi.dtype)
        kk = jnp.concatenate([kp, ki], axis=0)
        vv = jnp.concatenate([vp, vi], axis=0)
        return _sb_apply(qi[None], kk[None], vv[None], bias, q_idx, k_idx)[0]

    o = lax.map(one, (q, k, v, page_table))
    return o.reshape(b, t, D_SB)


def setup_inputs(seed: int = 0) -> dict:
    key = jax.random.key(seed)
    ks = iter(jax.random.split(key, 48))
    f32 = jnp.float32
    n_pages = PAST_LEN // PAGE_SIZE
    n_used = DEC_BATCH * n_pages
    n_pool = n_used + max(1, n_used // 4)

    def nrm(shape, scale):
        return jax.random.normal(next(ks), shape, f32) * scale

    def gain(shape):
        return 1.0 + nrm(shape, 0.1)

    d_in_ssd = D_INNER + CONV_DIM + SSD_HEADS + D_MEM
    u = jax.random.uniform(next(ks), (N_SSD_LAYERS, SSD_HEADS), f32)
    dt0 = jnp.exp(u * (math.log(0.1) - math.log(1e-3)) + math.log(1e-3))
    return {
        'x_prompt': nrm((BATCH, SEQ, D_MODEL), 1.0),
        'x_sample': nrm((DEC_BATCH, DEC_SEQ, D_MODEL), 1.0),
        'mem_prompt': nrm((BATCH, N_MEM, D_MODEL), 1.0),
        'state_ssm': nrm((N_SSD_LAYERS, DEC_BATCH, SSD_HEADS, SSD_HEAD_DIM, SSD_STATE), 0.5),
        'state_conv': nrm((N_SSD_LAYERS, DEC_BATCH, SSD_CONV - 1, CONV_DIM), 1.0),
        'cache_k': nrm((N_SB_LAYERS, n_pool, PAGE_SIZE, SB_HEADS, SB_HEAD_DIM), 1.0),
        'cache_v': nrm((N_SB_LAYERS, n_pool, PAGE_SIZE, SB_HEADS, SB_HEAD_DIM), 1.0),
        'cache_mem_k': nrm((DEPTH, DEC_BATCH, N_MEM, MEM_HEADS, MEM_HEAD_DIM), 1.0),
        'cache_mem_v': nrm((DEPTH, DEC_BATCH, N_MEM, MEM_HEADS, MEM_HEAD_DIM), 1.0),
        'page_table': jax.random.permutation(next(ks), n_pool)[:n_used].reshape(DEC_BATCH, n_pages).astype(jnp.int32),
        'g_ffn1': gain((DEPTH, D_MODEL)),
        'w_ffn1_in': nrm((DEPTH, D_MODEL, 2 * D_FF), D_MODEL ** -0.5),
        'w_ffn1_out': nrm((DEPTH, D_FF, D_MODEL), D_FF ** -0.5),
        'g_mix': gain((DEPTH, D_MODEL)),
        'g_mem': gain((DEPTH, D_MODEL)),
        'w_mem_kv': nrm((DEPTH, D_MODEL, 2 * D_MEM), D_MODEL ** -0.5),
        'w_in_ssd': nrm((N_SSD_LAYERS, D_MODEL, d_in_ssd), D_MODEL ** -0.5),
        'conv_w': nrm((N_SSD_LAYERS, SSD_CONV, CONV_DIM), SSD_CONV ** -0.5),
        'conv_b': nrm((N_SSD_LAYERS, CONV_DIM), 0.02),
        'dt_bias': dt0 + jnp.log(-jnp.expm1(-dt0)),
        'a_log': jnp.log(jax.random.uniform(next(ks), (N_SSD_LAYERS, SSD_HEADS), f32, minval=1.0, maxval=16.0)),
        'd_skip': gain((N_SSD_LAYERS, SSD_HEADS)),
        'g_ssd_norm': gain((N_SSD_LAYERS, D_INNER)),
        'w_out_ssd': nrm((N_SSD_LAYERS, D_INNER + D_MEM, D_MODEL), (D_INNER + D_MEM) ** -0.5),
        'w_in_sb': nrm((N_SB_LAYERS, D_MODEL, 3 * D_SB + D_MEM), D_MODEL ** -0.5),
        'sb_bias': SB_BIAS_INIT + nrm((N_SB_LAYERS, SB_HEADS), 0.1),
        'w_out_sb': nrm((N_SB_LAYERS, D_SB + D_MEM, D_MODEL), (D_SB + D_MEM) ** -0.5),
        'g_ffn2': gain((DEPTH, D_MODEL)),
        'w_ffn2_in': nrm((DEPTH, D_MODEL, 2 * D_FF), D_MODEL ** -0.5),
        'w_ffn2_out': nrm((DEPTH, D_FF, D_MODEL), D_FF ** -0.5),
        'g_final': gain((D_MODEL,)),
    }


def reference(x_prompt, x_sample, mem_prompt, state_ssm, state_conv, cache_k, cache_v, cache_mem_k, cache_mem_v,
              page_table, g_ffn1, w_ffn1_in, w_ffn1_out, g_mix, g_mem, w_mem_kv, w_in_ssd, conv_w, conv_b,
              dt_bias, a_log, d_skip, g_ssd_norm, w_out_ssd, w_in_sb, sb_bias, w_out_sb, g_ffn2, w_ffn2_in,
              w_ffn2_out, g_final):
    xp, xs = x_prompt, x_sample
    bp = x_prompt.shape[0]
    zero_conv = jnp.zeros((bp, SSD_CONV - 1, CONV_DIM), x_prompt.dtype)
    zero_ssm = jnp.zeros((bp, SSD_HEADS, SSD_HEAD_DIM, SSD_STATE), jnp.float32)
    p_ssm, p_conv, p_k, p_v, p_mk, p_mv = [], [], [], [], [], []
    s_ssm, s_conv, s_k, s_v = [], [], [], []
    for i in range(DEPTH):
        xp = xp + FFN_RES * _swiglu(_rmsnorm(xp, g_ffn1[i]), w_ffn1_in[i], w_ffn1_out[i])
        xs = xs + FFN_RES * _swiglu(_rmsnorm(xs, g_ffn1[i]), w_ffn1_in[i], w_ffn1_out[i])
        mkp, mvp = _mem_kv(mem_prompt, g_mem[i], w_mem_kv[i])
        p_mk.append(mkp)
        p_mv.append(mvp)
        mks, mvs = cache_mem_k[i], cache_mem_v[i]
        hp = _rmsnorm(xp, g_mix[i])
        hs = _rmsnorm(xs, g_mix[i])
        j = i // 2
        if i % 2 == 0:
            wts = (w_in_ssd[j], conv_w[j], conv_b[j], dt_bias[j], a_log[j], d_skip[j], g_ssd_norm[j], w_out_ssd[j])
            op, cbp, hfp = _ssd_mixer(hp, zero_conv, zero_ssm, mkp, mvp, *wts)
            os_, cbs, hfs = _ssd_mixer(hs, state_conv[j], state_ssm[j], mks, mvs, *wts)
            p_conv.append(cbp)
            p_ssm.append(hfp)
            s_conv.append(cbs)
            s_ssm.append(hfs)
        else:
            qp, kp, vp, qmp = _sb_project(hp, w_in_sb[j])
            op = jnp.concatenate([_sb_prompt(qp, kp, vp, sb_bias[j]), _mem_attend(qmp, mkp, mvp)], axis=-1) @ w_out_sb[j]
            qs, ks_, vs_, qms = _sb_project(hs, w_in_sb[j])
            o_sb = _sb_sample(qs, ks_, vs_, sb_bias[j], cache_k, cache_v, j, page_table)
            os_ = jnp.concatenate([o_sb, _mem_attend(qms, mks, mvs)], axis=-1) @ w_out_sb[j]
            p_k.append(kp)
            p_v.append(vp)
            s_k.append(ks_)
            s_v.append(vs_)
        xp = xp + op
        xs = xs + os_
        xp = xp + FFN_RES * _swiglu(_rmsnorm(xp, g_ffn2[i]), w_ffn2_in[i], w_ffn2_out[i])
        xs = xs + FFN_RES * _swiglu(_rmsnorm(xs, g_ffn2[i]), w_ffn2_in[i], w_ffn2_out[i])
    y_prompt = _rmsnorm(xp, g_final)
    y_sample = _rmsnorm(xs, g_final)
    return (y_prompt, y_sample, jnp.stack(p_ssm), jnp.stack(p_conv), jnp.stack(p_k), jnp.stack(p_v),
            jnp.stack(p_mk), jnp.stack(p_mv), jnp.stack(s_ssm), jnp.stack(s_conv), jnp.stack(s_k), jnp.stack(s_v))
```

```python
import functools

import jax
import jax.numpy as jnp
from jax import lax
from jax.experimental import pallas as pl
from jax.experimental.pallas import tpu as pltpu

F32 = jnp.float32
BF16 = jnp.bfloat16

D_MODEL = 2048
N_MEM = 256
MEM_HEADS = 4
MEM_HEAD_DIM = D_MODEL // MEM_HEADS
D_MEM = MEM_HEADS * MEM_HEAD_DIM
D_INNER = 2 * D_MODEL
SSD_HEAD_DIM = 64
SSD_HEADS = D_INNER // SSD_HEAD_DIM
SSD_GROUPS = 8
SSD_HPG = SSD_HEADS // SSD_GROUPS
SSD_STATE = 128
SSD_CONV = 4
SSD_CHUNK = 256
GROUP_DIM = D_INNER // SSD_GROUPS
GROUP_CONV = GROUP_DIM + 2 * SSD_STATE
CONV_DIM = D_INNER + 2 * SSD_GROUPS * SSD_STATE
SB_HEAD_DIM = 128
SB_HEADS = D_MODEL // SB_HEAD_DIM
D_SB = SB_HEADS * SB_HEAD_DIM
PAGE_SIZE = 128
FFN_RES = 0.5
EPS = 1e-6

VMEM_LIMIT_BYTES = 56 * 1024 * 1024
SUBLANES = 8

_NT = (((1,), (1,)), ((), ()))
_TN = (((0,), (0,)), ((), ()))


def _params(*semantics):
    return pltpu.CompilerParams(dimension_semantics=semantics, vmem_limit_bytes=VMEM_LIMIT_BYTES)


def _rms(x):
    return x * lax.rsqrt(jnp.mean(x * x, axis=-1, keepdims=True) + EPS)


def _silu(x):
    return x * jax.nn.sigmoid(x)


def _softplus(x):
    return jnp.maximum(x, 0.0) + jnp.log1p(jnp.exp(-jnp.abs(x)))


def _bdot(a, b, dims=None):
    a = a.astype(BF16)
    b = b.astype(BF16)
    if dims is None:
        return jnp.dot(a, b, preferred_element_type=F32)
    return lax.dot_general(a, b, dims, preferred_element_type=F32)


def _xdot(a, b):
    return jnp.dot(a, b, preferred_element_type=F32, precision=lax.Precision.HIGHEST)


def _ffn_kernel(x_ref, g_ref, wg_ref, wu_ref, wo_ref, gf_ref, o_ref, h_ref, *, n_ff, final_norm):
    j = pl.program_id(1)

    @pl.when(j == 0)
    def _():
        h_ref[...] = (_rms(x_ref[...]) * g_ref[...]).astype(BF16)
        o_ref[...] = jnp.zeros_like(o_ref)

    h = h_ref[...]
    gate = _bdot(h, wg_ref[...])
    up = _bdot(h, wu_ref[...])
    o_ref[...] += _bdot(_silu(gate) * up, wo_ref[...])

    @pl.when(j == n_ff - 1)
    def _():
        y = x_ref[...] + FFN_RES * o_ref[...]
        if final_norm:
            y = _rms(y) * gf_ref[...]
        o_ref[...] = y


def _ffn(x, g, w_in, w_out, layer, g_final=None, *, tm=1024, tf=256):
    t, d = x.shape
    d_ff = w_out.shape[1]
    tm = min(tm, t)
    n_ff = d_ff // tf
    final_norm = g_final is not None
    gf = g_final if final_norm else g
    return pl.pallas_call(
        functools.partial(_ffn_kernel, n_ff=n_ff, final_norm=final_norm),
        grid=(t // tm, n_ff),
        in_specs=[
            pl.BlockSpec((tm, d), lambda i, j: (i, 0), pipeline_mode=pl.Buffered(1)),
            pl.BlockSpec((1, d), lambda i, j: (0, 0)),
            pl.BlockSpec((None, d, tf), lambda i, j: (layer, 0, j)),
            pl.BlockSpec((None, d, tf), lambda i, j: (layer, 0, n_ff + j)),
            pl.BlockSpec((None, tf, d), lambda i, j: (layer, j, 0)),
            pl.BlockSpec((1, d), lambda i, j: (0, 0)),
        ],
        out_specs=pl.BlockSpec((tm, d), lambda i, j: (i, 0)),
        out_shape=jax.ShapeDtypeStruct((t, d), F32),
        scratch_shapes=[pltpu.VMEM((tm, d), BF16)],
        compiler_params=_params("parallel", "arbitrary"),
        name="ffn",
    )(x, g.reshape(1, d), w_in, w_in, w_out, gf.reshape(1, d))


def _proj_kernel(x_ref, g_ref, w_ref, o_ref, h_ref):
    @pl.when(pl.program_id(1) == 0)
    def _():
        h_ref[...] = (_rms(x_ref[...]) * g_ref[...]).astype(BF16)

    o_ref[...] = _bdot(h_ref[...], w_ref[...]).astype(o_ref.dtype)


def _proj(x, g, w, layer, col0, n, *, out_dtype=F32, tm=1024, tn=512):
    t, d = x.shape
    tm = min(tm, t)
    tn = min(tn, n)
    assert n % tn == 0 and col0 % tn == 0
    cb = col0 // tn
    return pl.pallas_call(
        _proj_kernel,
        grid=(t // tm, n // tn),
        in_specs=[
            pl.BlockSpec((tm, d), lambda i, j: (i, 0)),
            pl.BlockSpec((1, d), lambda i, j: (0, 0)),
            pl.BlockSpec((None, d, tn), lambda i, j: (layer, 0, cb + j)),
        ],
        out_specs=pl.BlockSpec((tm, tn), lambda i, j: (i, j)),
        out_shape=jax.ShapeDtypeStruct((t, n), out_dtype),
        scratch_shapes=[pltpu.VMEM((tm, d), BF16)],
        compiler_params=_params("parallel", "arbitrary"),
        name="proj",
    )(x, g.reshape(1, d), w)


def _outproj_kernel(x_ref, a1_ref, a2_ref, w1_ref, w2_ref, o_ref):
    o_ref[...] = x_ref[...] + _bdot(a1_ref[...], w1_ref[...]) + _bdot(a2_ref[...], w2_ref[...])


def _outproj(x, a1, a2, w, layer, *, tm=1024, tn=256):
    t, d = x.shape
    k1 = a1.shape[1]
    k2 = a2.shape[1]
    assert w.shape[1:] == (k1 + k2, d) and k1 % k2 == 0
    tm = min(tm, t)
    return pl.pallas_call(
        _outproj_kernel,
        grid=(t // tm, d // tn),
        in_specs=[
            pl.BlockSpec((tm, tn), lambda i, j: (i, j)),
            pl.BlockSpec((tm, k1), lambda i, j: (i, 0)),
            pl.BlockSpec((tm, k2), lambda i, j: (i, 0)),
            pl.BlockSpec((None, k1, tn), lambda i, j: (layer, 0, j)),
            pl.BlockSpec((None, k2, tn), lambda i, j: (layer, k1 // k2, j)),
        ],
        out_specs=pl.BlockSpec((tm, tn), lambda i, j: (i, j)),
        out_shape=jax.ShapeDtypeStruct((t, d), F32),
        compiler_params=_params("parallel", "arbitrary"),
        name="outproj",
    )(x, a1, a2, w, w)


def _memattn_kernel(q_ref, k_ref, v_ref, o_ref):
    scale = MEM_HEAD_DIM ** -0.5
    for h in range(MEM_HEADS):
        sl = slice(h * MEM_HEAD_DIM, (h + 1) * MEM_HEAD_DIM)
        s = _bdot(q_ref[:, sl], k_ref[:, sl], _NT) * scale
        p = jnp.exp(s - jnp.max(s, axis=-1, keepdims=True))
        p = p / jnp.sum(p, axis=-1, keepdims=True)
        o_ref[:, sl] = _bdot(p, v_ref[:, sl]).astype(o_ref.dtype)


def _memattn(q, mk, mv, layer, *, tq=512):
    b, t, _ = q.shape
    tq = min(tq, t)
    return pl.pallas_call(
        _memattn_kernel,
        grid=(b, t // tq),
        in_specs=[
            pl.BlockSpec((None, tq, D_MEM), lambda i, j: (i, j, 0)),
            pl.BlockSpec((None, None, N_MEM, D_MEM), lambda i, j: (layer, i, 0, 0)),
            pl.BlockSpec((None, None, N_MEM, D_MEM), lambda i, j: (layer, i, 0, 0)),
        ],
        out_specs=pl.BlockSpec((None, tq, D_MEM), lambda i, j: (i, j, 0)),
        out_shape=jax.ShapeDtypeStruct((b, t, D_MEM), BF16),
        compiler_params=_params("parallel", "arbitrary"),
        name="memattn",
    )(q, mk, mv)


def _ssd_kernel(z_ref, x_ref, b_ref, c_ref, dtc_ref, dtr_ref,
                cx0_ref, cb0_ref, cc0_ref, h0_ref,
                wx_ref, wb_ref, wc_ref, bx_ref, bb_ref, bc_ref,
                dbc_ref, dbr_ref, alc_ref, alr_ref, dsk_ref, gn_ref,
                y_ref, hfin_ref,
                xp_ref, bp_ref, cp_ref, h_ref, yd_ref, *, q, ng, n_chunks, zero_init):
    c = pl.program_id(2)
    R, P, N = SSD_HPG, SSD_HEAD_DIM, SSD_STATE
    pad = SUBLANES

    @pl.when(c == 0)
    def _():
        for ref, init in ((xp_ref, cx0_ref), (bp_ref, cb0_ref), (cp_ref, cc0_ref)):
            ref[0:pad, :] = jnp.zeros((pad, ref.shape[1]), F32)
            if not zero_init:
                ref[pad - (SSD_CONV - 1):pad, :] = init[...]
        if zero_init:
            h_ref[...] = jnp.zeros_like(h_ref)
        else:
            h_ref[...] = h0_ref[...]

    def conv(raw_ref, p_ref, w_ref, bias_ref):
        p_ref[pad:pad + q, :] = raw_ref[...]
        acc = bias_ref[...] + w_ref[SSD_CONV - 1:SSD_CONV, :] * p_ref[pad:pad + q, :]
        for k in range(SSD_CONV - 1):
            off = pad - (SSD_CONV - 1) + k
            acc = acc + w_ref[k:k + 1, :] * p_ref[off:off + q, :]
        p_ref[0:pad, :] = p_ref[q:q + pad, :]
        return _silu(acc)

    xc = conv(x_ref, xp_ref, wx_ref, bx_ref)
    bm = conv(b_ref, bp_ref, wb_ref, bb_ref)
    cm = conv(c_ref, cp_ref, wc_ref, bc_ref)

    dt_c = _softplus(dtc_ref[...] + dbc_ref[...])
    dt_r = _softplus(dtr_ref[...] + dbr_ref[...])
    da_c = dt_c * (-jnp.exp(alc_ref[...]))
    da_r = dt_r * (-jnp.exp(alr_ref[...]))
    ti = lax.broadcasted_iota(jnp.int32, (q, q), 0)
    si = lax.broadcasted_iota(jnp.int32, (q, q), 1)
    tril = si <= ti
    cs_c = _xdot(tril.astype(F32), da_c)
    cs_r = _xdot(da_r, (ti <= si).astype(F32))

    hi = lax.broadcasted_iota(jnp.int32, (R, R * P), 0)
    ci = lax.broadcasted_iota(jnp.int32, (R, R * P), 1)
    expand = ((ci >= hi * P) & (ci < (hi + 1) * P)).astype(F32)

    for gi in range(ng):
        gs = slice(gi * R * P, (gi + 1) * R * P)
        ns = slice(gi * N, (gi + 1) * N)
        hs = slice(gi * R, (gi + 1) * R)
        x_g, b_g, c_g = xc[:, gs], bm[:, ns], cm[:, ns]
        csx = _xdot(cs_c[:, hs], expand)
        xdt = x_g * _xdot(dt_c[:, hs], expand)
        cb = _bdot(c_g, b_g, _NT)

        if q <= SUBLANES:
            trow = lax.broadcasted_iota(jnp.int32, (q, R * P), 0)
            yd = jnp.zeros((q, R * P), F32)
            for s in range(q):
                w = jnp.exp(jnp.where(trow >= s, csx - csx[s:s + 1, :], -jnp.inf))
                yd = yd + (cb[:, s:s + 1] * w) * xdt[s:s + 1, :]
        else:
            for r in range(R):
                col = cs_c[:, gi * R + r:gi * R + r + 1]
                row = cs_r[gi * R + r:gi * R + r + 1, :]
                m = cb * jnp.exp(jnp.where(tril, col - row, -jnp.inf))
                yd_ref[:, r * P:(r + 1) * P] = _bdot(m, xdt[:, r * P:(r + 1) * P])
            yd = yd_ref[...]

        h_old = h_ref[gs, :]
        y_off = _bdot(c_g, h_old, _NT) * jnp.exp(csx)
        x_end = xdt * jnp.exp(csx[q - 1:q, :] - csx)
        st = _bdot(x_end, b_g, _TN)
        for r in range(R):
            rs = slice(gi * R * P + r * P, gi * R * P + (r + 1) * P)
            dec = jnp.exp(cs_r[gi * R + r:gi * R + r + 1, q - 1:q])
            h_ref[rs, :] = h_ref[rs, :] * dec + st[r * P:(r + 1) * P, :]

        y = yd + y_off + dsk_ref[:, gs] * x_g
        y = y * _silu(z_ref[:, gs])
        y_ref[:, gs] = (_rms(y) * gn_ref[:, gs]).astype(y_ref.dtype)

    @pl.when(c == n_chunks - 1)
    def _():
        hfin_ref[...] = h_ref[...]


def _ssd(z, xbc, dt, conv0, h0, conv_w, conv_b, dt_bias, a_log, d_skip, g_norm, *, q, ng):
    b, L, _ = z.shape
    G, R, P, N = SSD_GROUPS, SSD_HPG, SSD_HEAD_DIM, SSD_STATE
    zero_init = h0 is None
    if zero_init:
        conv0 = jnp.zeros((b, SSD_CONV - 1, CONV_DIM), F32)
        h0 = jnp.zeros((b, SUBLANES, N), F32)
    ngrp = G // ng
    n_chunks = L // q
    gw, nw, hw = ng * R * P, ng * N, ng * R
    b_blk0 = D_INNER // nw
    c_blk0 = (D_INNER + G * N) // nw

    dt_c = dt.reshape(b, L, ngrp, hw).transpose(0, 2, 1, 3)
    dt_r = dt_c.transpose(0, 1, 3, 2)
    dbc = dt_bias.reshape(ngrp, 1, hw)
    dbr = dt_bias.reshape(ngrp, hw, 1)
    alc = a_log.reshape(ngrp, 1, hw)
    alr = a_log.reshape(ngrp, hw, 1)
    dsk = jnp.repeat(d_skip, P).reshape(1, D_INNER)
    gn = g_norm.reshape(1, D_INNER)
    cbias = conv_b.reshape(1, CONV_DIM)

    seq = lambda bi, g, c: (bi, c, g)
    kern = functools.partial(_ssd_kernel, q=q, ng=ng, n_chunks=n_chunks, zero_init=zero_init)
    h0_spec = (pl.BlockSpec((None, SUBLANES, N), lambda bi, g, c: (bi, 0, 0)) if zero_init
               else pl.BlockSpec((None, gw, N), lambda bi, g, c: (bi, g, 0)))
    y, hfin = pl.pallas_call(
        kern,
        grid=(b, ngrp, n_chunks),
        in_specs=[
            pl.BlockSpec((None, q, gw), seq),
            pl.BlockSpec((None, q, gw), seq),
            pl.BlockSpec((None, q, nw), lambda bi, g, c: (bi, c, b_blk0 + g)),
            pl.BlockSpec((None, q, nw), lambda bi, g, c: (bi, c, c_blk0 + g)),
            pl.BlockSpec((None, None, q, hw), lambda bi, g, c: (bi, g, c, 0)),
            pl.BlockSpec((None, None, hw, q), lambda bi, g, c: (bi, g, 0, c)),
            pl.BlockSpec((None, SSD_CONV - 1, gw), lambda bi, g, c: (bi, 0, g)),
            pl.BlockSpec((None, SSD_CONV - 1, nw), lambda bi, g, c: (bi, 0, b_blk0 + g)),
            pl.BlockSpec((None, SSD_CONV - 1, nw), lambda bi, g, c: (bi, 0, c_blk0 + g)),
            h0_spec,
            pl.BlockSpec((SSD_CONV, gw), lambda bi, g, c: (0, g)),
            pl.BlockSpec((SSD_CONV, nw), lambda bi, g, c: (0, b_blk0 + g)),
            pl.BlockSpec((SSD_CONV, nw), lambda bi, g, c: (0, c_blk0 + g)),
            pl.BlockSpec((1, gw), lambda bi, g, c: (0, g)),
            pl.BlockSpec((1, nw), lambda bi, g, c: (0, b_blk0 + g)),
            pl.BlockSpec((1, nw), lambda bi, g, c: (0, c_blk0 + g)),
            pl.BlockSpec((None, 1, hw), lambda bi, g, c: (g, 0, 0)),
            pl.BlockSpec((None, hw, 1), lambda bi, g, c: (g, 0, 0)),
            pl.BlockSpec((None, 1, hw), lambda bi, g, c: (g, 0, 0)),
            pl.BlockSpec((None, hw, 1), lambda bi, g, c: (g, 0, 0)),
            pl.BlockSpec((1, gw), lambda bi, g, c: (0, g)),
            pl.BlockSpec((1, gw), lambda bi, g, c: (0, g)),
        ],
        out_specs=[
            pl.BlockSpec((None, q, gw), seq),
            pl.BlockSpec((None, gw, N), lambda bi, g, c: (bi, g, 0)),
        ],
        out_shape=[
            jax.ShapeDtypeStruct((b, L, D_INNER), BF16),
            jax.ShapeDtypeStruct((b, SSD_HEADS * P, N), F32),
        ],
        scratch_shapes=[
            pltpu.VMEM((q + SUBLANES, gw), F32),
            pltpu.VMEM((q + SUBLANES, nw), F32),
            pltpu.VMEM((q + SUBLANES, nw), F32),
            pltpu.VMEM((gw, N), F32),
            pltpu.VMEM((q, R * P), F32),
        ],
        compiler_params=_params("parallel", "parallel", "arbitrary"),
        name="ssd",
    )(z, xbc, xbc, xbc, dt_c, dt_r, conv0, conv0, conv0, h0,
      conv_w, conv_w, conv_w, cbias, cbias, cbias, dbc, dbr, alc, alr, dsk, gn)
    return y, hfin


def _sb_block(q, k, v, bias, carry, acc, mask):
    n = k.shape[0]
    z = _bdot(q, k, _NT) * (SB_HEAD_DIM ** -0.5) + bias
    log_keep = -_softplus(z)
    if mask is not None:
        log_keep = jnp.where(mask, log_keep, 0.0)
    si = lax.broadcasted_iota(jnp.int32, (n, n), 0)
    ji = lax.broadcasted_iota(jnp.int32, (n, n), 1)
    later = (si > ji).astype(BF16)
    hi = log_keep.astype(BF16)
    lo = (log_keep - hi.astype(F32)).astype(BF16)
    after = (jnp.dot(hi, later, preferred_element_type=F32)
             + jnp.dot(lo, later, preferred_element_type=F32)) + carry
    attn = jnp.exp((z + log_keep) + after)
    if mask is not None:
        attn = jnp.where(mask, attn, 0.0)
    acc = acc + _bdot(attn, v)
    carry = carry + jnp.sum(log_keep, axis=-1, keepdims=True)
    return carry, acc


def _sbp_kernel(bias_ref, q_ref, k_ref, v_ref, o_ref, *, tq):
    qi = pl.program_id(2)
    q = q_ref[...].astype(BF16)
    bias = bias_ref[...]
    ti = lax.broadcasted_iota(jnp.int32, (tq, tq), 0)
    ji = lax.broadcasted_iota(jnp.int32, (tq, tq), 1)
    start = pl.multiple_of(qi * tq, tq)
    carry, acc = _sb_block(q, k_ref[pl.ds(start, tq), :], v_ref[pl.ds(start, tq), :], bias,
                           jnp.zeros((tq, 1), F32), jnp.zeros((tq, SB_HEAD_DIM), F32), ji < ti)

    def body(i, state):
        s0 = pl.multiple_of((qi - 1 - i) * tq, tq)
        return _sb_block(q, k_ref[pl.ds(s0, tq), :], v_ref[pl.ds(s0, tq), :], bias, *state, None)

    carry, acc = lax.fori_loop(0, qi, body, (carry, acc))
    o_ref[...] = acc.astype(o_ref.dtype)


def _sb_prompt(q, k, v, bias, *, tq=256):
    b, L, _ = q.shape
    dh = SB_HEAD_DIM
    return pl.pallas_call(
        functools.partial(_sbp_kernel, tq=tq),
        grid=(b, SB_HEADS, L // tq),
        in_specs=[
            pl.BlockSpec((None, 1, 1), lambda bi, h, i: (h, 0, 0)),
            pl.BlockSpec((None, tq, dh), lambda bi, h, i: (bi, i, h)),
            pl.BlockSpec((None, L, dh), lambda bi, h, i: (bi, 0, h)),
            pl.BlockSpec((None, L, dh), lambda bi, h, i: (bi, 0, h)),
        ],
        out_specs=pl.BlockSpec((None, tq, dh), lambda bi, h, i: (bi, i, h)),
        out_shape=jax.ShapeDtypeStruct((b, L, D_SB), BF16),
        compiler_params=_params("parallel", "parallel", "arbitrary"),
        name="sb_prompt",
    )(bias.reshape(SB_HEADS, 1, 1), q, k, v)


def _sbs_kernel(pt_ref, bias_ref, q_ref, kn_ref, vn_ref, kp_ref, vp_ref, o_ref,
                qbd_ref, acc_ref, carry_ref, *, t, n_pages):
    del pt_ref
    p = pl.program_id(1)
    m = SB_HEADS * t
    rows = lax.broadcasted_iota(jnp.int32, (m, D_SB), 0)
    cols = lax.broadcasted_iota(jnp.int32, (m, D_SB), 1)
    same_head = (rows // t) == (cols // SB_HEAD_DIM)

    @pl.when(p == 0)
    def _():
        qt = jnp.concatenate([q_ref[...]] * SB_HEADS, axis=0)
        qbd = jnp.where(same_head, qt, 0.0).astype(BF16)
        qbd_ref[...] = qbd
        npad = PAGE_SIZE - t
        kn = jnp.concatenate([kn_ref[...], jnp.zeros((npad, D_SB), F32)], axis=0)
        vn = jnp.concatenate([vn_ref[...], jnp.zeros((npad, D_SB), F32)], axis=0)
        ri = lax.broadcasted_iota(jnp.int32, (m, PAGE_SIZE), 0)
        ki = lax.broadcasted_iota(jnp.int32, (m, PAGE_SIZE), 1)
        carry, acc = _sb_block(qbd, kn, vn, bias_ref[...], jnp.zeros((m, 1), F32),
                               jnp.zeros((m, D_SB), F32), ki < (ri % t))
        carry_ref[...] = carry
        acc_ref[...] = acc

    carry, acc = _sb_block(qbd_ref[...], kp_ref[...], vp_ref[...], bias_ref[...],
                           carry_ref[...], acc_ref[...], None)
    carry_ref[...] = carry
    acc_ref[...] = acc

    @pl.when(p == n_pages - 1)
    def _():
        for h in range(SB_HEADS):
            o_ref[:, h * SB_HEAD_DIM:(h + 1) * SB_HEAD_DIM] = (
                acc_ref[h * t:(h + 1) * t, h * SB_HEAD_DIM:(h + 1) * SB_HEAD_DIM].astype(o_ref.dtype))


def _sb_sample(q, k, v, bias, cache_k, cache_v, layer, page_table):
    b, t, _ = q.shape
    n_pages = page_table.shape[1]
    m = SB_HEADS * t
    bias_rows = jnp.repeat(bias, t).reshape(m, 1)
    page = lambda bi, p, pt: (layer, pt[bi, n_pages - 1 - p], 0, 0)
    tok = lambda bi, p, pt: (bi, 0, 0)
    return pl.pallas_call(
        functools.partial(_sbs_kernel, t=t, n_pages=n_pages),
        grid_spec=pltpu.PrefetchScalarGridSpec(
            num_scalar_prefetch=1,
            grid=(b, n_pages),
            in_specs=[
                pl.BlockSpec((m, 1), lambda bi, p, pt: (0, 0)),
                pl.BlockSpec((None, t, D_SB), tok),
                pl.BlockSpec((None, t, D_SB), tok),
                pl.BlockSpec((None, t, D_SB), tok),
                pl.BlockSpec((None, None, PAGE_SIZE, D_SB), page),
                pl.BlockSpec((None, None, PAGE_SIZE, D_SB), page),
            ],
            out_specs=pl.BlockSpec((None, t, D_SB), tok),
            scratch_shapes=[
                pltpu.VMEM((m, D_SB), BF16),
                pltpu.VMEM((m, D_SB), F32),
                pltpu.VMEM((m, 1), F32),
            ],
        ),
        out_shape=jax.ShapeDtypeStruct((b, t, D_SB), BF16),
        compiler_params=_params("parallel", "arbitrary"),
        name="sb_sample",
    )(page_table, bias_rows, q, k, v, cache_k, cache_v)


def kernel(x_prompt, x_sample, mem_prompt, state_ssm, state_conv, cache_k, cache_v, cache_mem_k, cache_mem_v,
           page_table, g_ffn1, w_ffn1_in, w_ffn1_out, g_mix, g_mem, w_mem_kv, w_in_ssd, conv_w, conv_b,
           dt_bias, a_log, d_skip, g_ssd_norm, w_out_ssd, w_in_sb, sb_bias, w_out_sb, g_ffn2, w_ffn2_in,
           w_ffn2_out, g_final):
    bp, lp, d = x_prompt.shape
    bs, ls, _ = x_sample.shape
    depth = g_ffn1.shape[0]
    H, P, N = SSD_HEADS, SSD_HEAD_DIM, SSD_STATE
    groups = ((bp, lp), (bs, ls))
    xs = [x_prompt.reshape(bp * lp, d), x_sample.reshape(bs * ls, d)]
    mem2d = mem_prompt.reshape(bp * N_MEM, d)
    ck = cache_k.reshape(cache_k.shape[:3] + (D_SB,))
    cv = cache_v.reshape(cache_v.shape[:3] + (D_SB,))

    p_ssm, p_conv, p_k, p_v, p_mk, p_mv = [], [], [], [], [], []
    s_ssm, s_conv, s_k, s_v = [], [], [], []
    for i in range(depth):
        j = i // 2
        last = i == depth - 1
        xs = [_ffn(x, g_ffn1[i], w_ffn1_in, w_ffn1_out, i) for x in xs]

        mk = _proj(mem2d, g_mem[i], w_mem_kv, i, 0, D_MEM)
        mv = _proj(mem2d, g_mem[i], w_mem_kv, i, D_MEM, D_MEM)
        p_mk.append(mk.reshape(bp, N_MEM, MEM_HEADS, MEM_HEAD_DIM))
        p_mv.append(mv.reshape(bp, N_MEM, MEM_HEADS, MEM_HEAD_DIM))
        mem_kv = ((mk.reshape(1, bp, N_MEM, D_MEM), mv.reshape(1, bp, N_MEM, D_MEM), 0),
                  (cache_mem_k.reshape(depth, bs, N_MEM, D_MEM), cache_mem_v.reshape(depth, bs, N_MEM, D_MEM), i))

        new_xs = []
        for gi, (x, (b, L)) in enumerate(zip(xs, groups)):
            prompt = gi == 0
            g = g_mix[i]
            if i % 2 == 0:
                assert L >= SSD_CONV - 1
                dt0 = D_INNER + CONV_DIM
                z = _proj(x, g, w_in_ssd, j, 0, D_INNER)
                xbc = _proj(x, g, w_in_ssd, j, D_INNER, CONV_DIM)
                dt = _proj(x, g, w_in_ssd[j:j + 1, :, dt0:dt0 + H], 0, 0, H)
                qm = _proj(x, g, w_in_ssd[j:j + 1, :, dt0 + H:], 0, 0, D_MEM, out_dtype=BF16)
                xbc3 = xbc.reshape(b, L, CONV_DIM)
                if prompt:
                    conv0, h0, q, ng = None, None, SSD_CHUNK, 1
                else:
                    conv0, h0, q, ng = state_conv[j], state_ssm[j].reshape(b, H * P, N), L, SSD_GROUPS
                y, hfin = _ssd(z.reshape(b, L, D_INNER), xbc3, dt.reshape(b, L, H), conv0, h0,
                               conv_w[j], conv_b[j], dt_bias[j], a_log[j], d_skip[j], g_ssd_norm[j],
                               q=q, ng=ng)
                (p_conv if prompt else s_conv).append(xbc3[:, L - (SSD_CONV - 1):, :])
                (p_ssm if prompt else s_ssm).append(hfin.reshape(b, H, P, N))
                a1, w_out = y.reshape(b * L, D_INNER), w_out_ssd
            else:
                qh = _proj(x, g, w_in_sb, j, 0, D_SB, out_dtype=BF16)
                kh = _proj(x, g, w_in_sb, j, D_SB, D_SB)
                vh = _proj(x, g, w_in_sb, j, 2 * D_SB, D_SB)
                qm = _proj(x, g, w_in_sb, j, 3 * D_SB, D_MEM, out_dtype=BF16)
                q3, k3, v3 = (a.reshape(b, L, D_SB) for a in (qh, kh, vh))
                if prompt:
                    o = _sb_prompt(q3, k3, v3, sb_bias[j])
                else:
                    o = _sb_sample(q3, k3, v3, sb_bias[j], ck, cv, j, page_table)
                (p_k if prompt else s_k).append(kh.reshape(b, L, SB_HEADS, SB_HEAD_DIM))
                (p_v if prompt else s_v).append(vh.reshape(b, L, SB_HEADS, SB_HEAD_DIM))
                a1, w_out = o.reshape(b * L, D_SB), w_out_sb
            o_mem = _memattn(qm.reshape(b, L, D_MEM), *mem_kv[gi])
            x = _outproj(x, a1, o_mem.reshape(b * L, D_MEM), w_out, j)
            x = _ffn(x, g_ffn2[i], w_ffn2_in, w_ffn2_out, i, g_final if last else None)
            new_xs.append(x)
        xs = new_xs

    return (xs[0].reshape(bp, lp, d), xs[1].reshape(bs, ls, d),
            jnp.stack(p_ssm), jnp.stack(p_conv), jnp.stack(p_k), jnp.stack(p_v),
            jnp.stack(p_mk), jnp.stack(p_mv), jnp.stack(s_ssm), jnp.stack(s_conv), jnp.stack(s_k), jnp.stack(s_v))
```

```python
import functools

import jax
import jax.numpy as jnp
from jax import lax
from jax.experimental import pallas as pl
from jax.experimental.pallas import tpu as pltpu

F32 = jnp.float32
BF16 = jnp.bfloat16

D_MODEL = 2048
N_MEM = 256
MEM_HEADS = 4
MEM_HEAD_DIM = D_MODEL // MEM_HEADS
D_MEM = MEM_HEADS * MEM_HEAD_DIM
D_INNER = 2 * D_MODEL
SSD_HEAD_DIM = 64
SSD_HEADS = D_INNER // SSD_HEAD_DIM
SSD_GROUPS = 8
SSD_HPG = SSD_HEADS // SSD_GROUPS
SSD_STATE = 128
SSD_CONV = 4
SSD_CHUNK = 256
GROUP_DIM = D_INNER // SSD_GROUPS
GROUP_CONV = GROUP_DIM + 2 * SSD_STATE
CONV_DIM = D_INNER + 2 * SSD_GROUPS * SSD_STATE
SB_HEAD_DIM = 128
SB_HEADS = D_MODEL // SB_HEAD_DIM
D_SB = SB_HEADS * SB_HEAD_DIM
PAGE_SIZE = 128
FFN_RES = 0.5
EPS = 1e-6

VMEM_LIMIT_BYTES = 56 * 1024 * 1024
SUBLANES = 8

_NT = (((1,), (1,)), ((), ()))
_TN = (((0,), (0,)), ((), ()))


def _params(*semantics):
    return pltpu.CompilerParams(dimension_semantics=semantics, vmem_limit_bytes=VMEM_LIMIT_BYTES)


def _rms(x):
    return x * lax.rsqrt(jnp.mean(x * x, axis=-1, keepdims=True) + EPS)


def _silu(x):
    return x * jax.nn.sigmoid(x)


def _softplus(x):
    return jnp.maximum(x, 0.0) + jnp.log1p(jnp.exp(-jnp.abs(x)))


def _bdot(a, b, dims=None):
    a = a.astype(BF16)
    b = b.astype(BF16)
    if dims is None:
        return jnp.dot(a, b, preferred_element_type=F32)
    return lax.dot_general(a, b, dims, preferred_element_type=F32)


def _split3(x):
    x1 = x.astype(BF16)
    r1 = x - x1.astype(F32)
    x2 = r1.astype(BF16)
    return x1, x2, (r1 - x2.astype(F32)).astype(BF16)


def _dot01(a, b, *, exact):
    if exact == "a":
        return sum(jnp.dot(a.astype(BF16), p, preferred_element_type=F32) for p in _split3(b))
    return sum(jnp.dot(p, b.astype(BF16), preferred_element_type=F32) for p in _split3(a))


def _ffn_kernel(x_ref, g_ref, wg_ref, wu_ref, wo_ref, gf_ref, o_ref, h_ref, *, n_ff, final_norm):
    j = pl.program_id(1)

    @pl.when(j == 0)
    def _():
        h_ref[...] = (_rms(x_ref[...]) * g_ref[...]).astype(BF16)
        o_ref[...] = jnp.zeros_like(o_ref)

    h = h_ref[...]
    gate = _bdot(h, wg_ref[...])
    up = _bdot(h, wu_ref[...])
    o_ref[...] += _bdot(_silu(gate) * up, wo_ref[...])

    @pl.when(j == n_ff - 1)
    def _():
        y = x_ref[...] + FFN_RES * o_ref[...]
        if final_norm:
            y = _rms(y) * gf_ref[...]
        o_ref[...] = y


def _ffn(x, g, w_in, w_out, layer, g_final=None, *, tm=1024, tf=256):
    t, d = x.shape
    d_ff = w_out.shape[1]
    tm = min(tm, t)
    n_ff = d_ff // tf
    final_norm = g_final is not None
    gf = g_final if final_norm else g
    return pl.pallas_call(
        functools.partial(_ffn_kernel, n_ff=n_ff, final_norm=final_norm),
        grid=(t // tm, n_ff),
        in_specs=[
            pl.BlockSpec((tm, d), lambda i, j: (i, 0), pipeline_mode=pl.Buffered(1)),
            pl.BlockSpec((1, d), lambda i, j: (0, 0)),
            pl.BlockSpec((None, d, tf), lambda i, j: (layer, 0, j)),
            pl.BlockSpec((None, d, tf), lambda i, j: (layer, 0, n_ff + j)),
            pl.BlockSpec((None, tf, d), lambda i, j: (layer, j, 0)),
            pl.BlockSpec((1, d), lambda i, j: (0, 0)),
        ],
        out_specs=pl.BlockSpec((tm, d), lambda i, j: (i, 0)),
        out_shape=jax.ShapeDtypeStruct((t, d), F32),
        scratch_shapes=[pltpu.VMEM((tm, d), BF16)],
        compiler_params=_params("parallel", "arbitrary"),
        name="ffn",
    )(x, g.reshape(1, d), w_in, w_in, w_out, gf.reshape(1, d))


def _proj_kernel(x_ref, g_ref, w_ref, o_ref, h_ref):
    @pl.when(pl.program_id(1) == 0)
    def _():
        h_ref[...] = (_rms(x_ref[...]) * g_ref[...]).astype(BF16)

    o_ref[...] = _bdot(h_ref[...], w_ref[...]).astype(o_ref.dtype)


def _proj(x, g, w, layer, col0, n, *, out_dtype=F32, tm=1024, tn=512):
    t, d = x.shape
    tm = min(tm, t)
    tn = min(tn, n)
    assert n % tn == 0 and col0 % tn == 0
    cb = col0 // tn
    return pl.pallas_call(
        _proj_kernel,
        grid=(t // tm, n // tn),
        in_specs=[
            pl.BlockSpec((tm, d), lambda i, j: (i, 0)),
            pl.BlockSpec((1, d), lambda i, j: (0, 0)),
            pl.BlockSpec((None, d, tn), lambda i, j: (layer, 0, cb + j)),
        ],
        out_specs=pl.BlockSpec((tm, tn), lambda i, j: (i, j)),
        out_shape=jax.ShapeDtypeStruct((t, n), out_dtype),
        scratch_shapes=[pltpu.VMEM((tm, d), BF16)],
        compiler_params=_params("parallel", "arbitrary"),
        name="proj",
    )(x, g.reshape(1, d), w)


def _outproj_kernel(x_ref, a1_ref, a2_ref, w1_ref, w2_ref, o_ref):
    o_ref[...] = x_ref[...] + _bdot(a1_ref[...], w1_ref[...]) + _bdot(a2_ref[...], w2_ref[...])


def _outproj(x, a1, a2, w, layer, *, tm=1024, tn=256):
    t, d = x.shape
    k1 = a1.shape[1]
    k2 = a2.shape[1]
    assert w.shape[1:] == (k1 + k2, d) and k1 % k2 == 0
    tm = min(tm, t)
    return pl.pallas_call(
        _outproj_kernel,
        grid=(t // tm, d // tn),
        in_specs=[
            pl.BlockSpec((tm, tn), lambda i, j: (i, j)),
            pl.BlockSpec((tm, k1), lambda i, j: (i, 0)),
            pl.BlockSpec((tm, k2), lambda i, j: (i, 0)),
            pl.BlockSpec((None, k1, tn), lambda i, j: (layer, 0, j)),
            pl.BlockSpec((None, k2, tn), lambda i, j: (layer, k1 // k2, j)),
        ],
        out_specs=pl.BlockSpec((tm, tn), lambda i, j: (i, j)),
        out_shape=jax.ShapeDtypeStruct((t, d), F32),
        compiler_params=_params("parallel", "arbitrary"),
        name="outproj",
    )(x, a1, a2, w, w)


def _mem_head(q, k, v):
    s = _bdot(q, k, _NT) * (MEM_HEAD_DIM ** -0.5)
    p = jnp.exp(s - jnp.max(s, axis=-1, keepdims=True))
    p = p / jnp.sum(p, axis=-1, keepdims=True)
    return _bdot(p, v)


def _memattn_kernel(q_ref, k_ref, v_ref, o_ref):
    for h in range(MEM_HEADS):
        sl = slice(h * MEM_HEAD_DIM, (h + 1) * MEM_HEAD_DIM)
        o_ref[:, sl] = _mem_head(q_ref[:, sl], k_ref[:, sl], v_ref[:, sl]).astype(o_ref.dtype)


def _memattn(q, mk, mv, *, tq=512):
    b, t, _ = q.shape
    tq = min(tq, t)
    kv_spec = pl.BlockSpec((None, N_MEM, D_MEM), lambda i, j: (i, 0, 0))
    return pl.pallas_call(
        _memattn_kernel,
        grid=(b, t // tq),
        in_specs=[pl.BlockSpec((None, tq, D_MEM), lambda i, j: (i, j, 0)), kv_spec, kv_spec],
        out_specs=pl.BlockSpec((None, tq, D_MEM), lambda i, j: (i, j, 0)),
        out_shape=jax.ShapeDtypeStruct((b, t, D_MEM), BF16),
        compiler_params=_params("parallel", "arbitrary"),
        name="memattn",
    )(q, mk, mv)


def _memattn_cache_kernel(q_ref, k_hbm, v_hbm, o_ref, kbuf, vbuf, sem, *, bb, layer):
    i = pl.program_id(0)
    slot = i % 2

    def head_copies(stp, slt):
        cps = []
        for bi in range(bb):
            for h in range(MEM_HEADS):
                for hbm, buf in ((k_hbm, kbuf), (v_hbm, vbuf)):
                    cps.append(pltpu.make_async_copy(
                        hbm.at[layer, stp * bb + bi, :, h, :], buf.at[slt, bi, h], sem.at[slt]))
        return cps

    @pl.when(i == 0)
    def _():
        for cp in head_copies(0, 0):
            cp.start()

    @pl.when(i + 1 < pl.num_programs(0))
    def _():
        for cp in head_copies(i + 1, 1 - slot):
            cp.start()

    for cp in head_copies(i, slot):
        cp.wait()
    for bi in range(bb):
        for h in range(MEM_HEADS):
            sl = slice(h * MEM_HEAD_DIM, (h + 1) * MEM_HEAD_DIM)
            o_ref[bi, :, sl] = _mem_head(q_ref[bi, :, sl], kbuf[slot, bi, h], vbuf[slot, bi, h]).astype(o_ref.dtype)


def _memattn_cache(q, cache_k, cache_v, layer, *, bb=2):
    b, t, _ = q.shape
    assert b % bb == 0
    stage = pltpu.VMEM((2, bb, MEM_HEADS, N_MEM, MEM_HEAD_DIM), F32)
    return pl.pallas_call(
        functools.partial(_memattn_cache_kernel, bb=bb, layer=layer),
        grid=(b // bb,),
        in_specs=[pl.BlockSpec((bb, t, D_MEM), lambda i: (i, 0, 0)),
                  pl.BlockSpec(memory_space=pl.ANY), pl.BlockSpec(memory_space=pl.ANY)],
        out_specs=pl.BlockSpec((bb, t, D_MEM), lambda i: (i, 0, 0)),
        out_shape=jax.ShapeDtypeStruct((b, t, D_MEM), BF16),
        scratch_shapes=[stage, stage, pltpu.SemaphoreType.DMA((2,))],
        compiler_params=_params("arbitrary"),
        name="memattn_cache",
    )(q, cache_k, cache_v)


def _ssd_kernel(z_ref, x_ref, b_ref, c_ref, dtc_ref, dtr_ref,
                cx0_ref, cb0_ref, cc0_ref, h0_ref,
                wx_ref, wb_ref, wc_ref, bx_ref, bb_ref, bc_ref,
                dbc_ref, dbr_ref, alc_ref, alr_ref, dsk_ref, gn_ref,
                y_ref, hfin_ref,
                xp_ref, bp_ref, cp_ref, h_ref, yd_ref, *, q, ng, n_chunks, zero_init):
    c = pl.program_id(2)
    R, P, N = SSD_HPG, SSD_HEAD_DIM, SSD_STATE
    pad = SUBLANES

    @pl.when(c == 0)
    def _():
        for ref, init in ((xp_ref, cx0_ref), (bp_ref, cb0_ref), (cp_ref, cc0_ref)):
            ref[0:pad, :] = jnp.zeros((pad, ref.shape[1]), F32)
            if not zero_init:
                ref[pad - (SSD_CONV - 1):pad, :] = init[...]
        if zero_init:
            h_ref[...] = jnp.zeros_like(h_ref)
        else:
            h_ref[...] = h0_ref[...]

    def conv(raw_ref, p_ref, w_ref, bias_ref):
        p_ref[pad:pad + q, :] = raw_ref[...]
        acc = bias_ref[...] + w_ref[SSD_CONV - 1:SSD_CONV, :] * p_ref[pad:pad + q, :]
        for k in range(SSD_CONV - 1):
            off = pad - (SSD_CONV - 1) + k
            acc = acc + w_ref[k:k + 1, :] * p_ref[off:off + q, :]
        p_ref[0:pad, :] = p_ref[q:q + pad, :]
        return _silu(acc)

    xc = conv(x_ref, xp_ref, wx_ref, bx_ref)
    bm = conv(b_ref, bp_ref, wb_ref, bb_ref)
    cm = conv(c_ref, cp_ref, wc_ref, bc_ref)

    dt_c = _softplus(dtc_ref[...] + dbc_ref[...])
    dt_r = _softplus(dtr_ref[...] + dbr_ref[...])
    da_c = dt_c * (-jnp.exp(alc_ref[...]))
    da_r = dt_r * (-jnp.exp(alr_ref[...]))
    ti = lax.broadcasted_iota(jnp.int32, (q, q), 0)
    si = lax.broadcasted_iota(jnp.int32, (q, q), 1)
    tril = si <= ti
    cs_c = _dot01(tril, da_c, exact="a")
    cs_r = _dot01(da_r, ti <= si, exact="b")

    hi = lax.broadcasted_iota(jnp.int32, (R, R * P), 0)
    ci = lax.broadcasted_iota(jnp.int32, (R, R * P), 1)
    expand = ((ci >= hi * P) & (ci < (hi + 1) * P)).astype(F32)

    for gi in range(ng):
        gs = slice(gi * R * P, (gi + 1) * R * P)
        ns = slice(gi * N, (gi + 1) * N)
        hs = slice(gi * R, (gi + 1) * R)
        x_g, b_g, c_g = xc[:, gs], bm[:, ns], cm[:, ns]
        csx = _dot01(cs_c[:, hs], expand, exact="b")
        xdt = x_g * _dot01(dt_c[:, hs], expand, exact="b")
        cb = _bdot(c_g, b_g, _NT)

        if q <= SUBLANES:
            trow = lax.broadcasted_iota(jnp.int32, (q, R * P), 0)
            yd = jnp.zeros((q, R * P), F32)
            for s in range(q):
                w = jnp.exp(jnp.where(trow >= s, csx - csx[s:s + 1, :], -jnp.inf))
                yd = yd + (cb[:, s:s + 1] * w) * xdt[s:s + 1, :]
        else:
            for r in range(R):
                col = cs_c[:, gi * R + r:gi * R + r + 1]
                row = cs_r[gi * R + r:gi * R + r + 1, :]
                m = cb * jnp.exp(jnp.where(tril, col - row, -jnp.inf))
                yd_ref[:, r * P:(r + 1) * P] = _bdot(m, xdt[:, r * P:(r + 1) * P])
            yd = yd_ref[...]

        h_old = h_ref[gs, :]
        y_off = _bdot(c_g, h_old, _NT) * jnp.exp(csx)
        x_end = xdt * jnp.exp(csx[q - 1:q, :] - csx)
        st = _bdot(x_end, b_g, _TN)
        for r in range(R):
            rs = slice(gi * R * P + r * P, gi * R * P + (r + 1) * P)
            dec = jnp.exp(cs_r[gi * R + r:gi * R + r + 1, q - 1:q])
            h_ref[rs, :] = h_ref[rs, :] * dec + st[r * P:(r + 1) * P, :]

        y = yd + y_off + dsk_ref[:, gs] * x_g
        y = y * _silu(z_ref[:, gs])
        y_ref[:, gs] = (_rms(y) * gn_ref[:, gs]).astype(y_ref.dtype)

    @pl.when(c == n_chunks - 1)
    def _():
        hfin_ref[...] = h_ref[...]


def _ssd(z, xbc, dt, conv0, h0, conv_w, conv_b, dt_bias, a_log, d_skip, g_norm, *, q, ng):
    b, L, _ = z.shape
    G, R, P, N = SSD_GROUPS, SSD_HPG, SSD_HEAD_DIM, SSD_STATE
    zero_init = h0 is None
    if zero_init:
        conv0 = jnp.zeros((b, SSD_CONV - 1, CONV_DIM), F32)
        h0 = jnp.zeros((b, SUBLANES, N), F32)
    ngrp = G // ng
    n_chunks = L // q
    gw, nw, hw = ng * R * P, ng * N, ng * R
    b_blk0 = D_INNER // nw
    c_blk0 = (D_INNER + G * N) // nw

    dt_c = dt.reshape(b, L, ngrp, hw).transpose(0, 2, 1, 3)
    dt_r = dt_c.transpose(0, 1, 3, 2)
    dbc = dt_bias.reshape(ngrp, 1, hw)
    dbr = dt_bias.reshape(ngrp, hw, 1)
    alc = a_log.reshape(ngrp, 1, hw)
    alr = a_log.reshape(ngrp, hw, 1)
    dsk = jnp.repeat(d_skip, P).reshape(1, D_INNER)
    gn = g_norm.reshape(1, D_INNER)
    cbias = conv_b.reshape(1, CONV_DIM)

    seq = lambda bi, g, c: (bi, c, g)
    kern = functools.partial(_ssd_kernel, q=q, ng=ng, n_chunks=n_chunks, zero_init=zero_init)
    h0_spec = (pl.BlockSpec((None, SUBLANES, N), lambda bi, g, c: (bi, 0, 0)) if zero_init
               else pl.BlockSpec((None, gw, N), lambda bi, g, c: (bi, g, 0)))
    y, hfin = pl.pallas_call(
        kern,
        grid=(b, ngrp, n_chunks),
        in_specs=[
            pl.BlockSpec((None, q, gw), seq),
            pl.BlockSpec((None, q, gw), seq),
            pl.BlockSpec((None, q, nw), lambda bi, g, c: (bi, c, b_blk0 + g)),
            pl.BlockSpec((None, q, nw), lambda bi, g, c: (bi, c, c_blk0 + g)),
            pl.BlockSpec((None, None, q, hw), lambda bi, g, c: (bi, g, c, 0)),
            pl.BlockSpec((None, None, hw, q), lambda bi, g, c: (bi, g, 0, c)),
            pl.BlockSpec((None, SSD_CONV - 1, gw), lambda bi, g, c: (bi, 0, g)),
            pl.BlockSpec((None, SSD_CONV - 1, nw), lambda bi, g, c: (bi, 0, b_blk0 + g)),
            pl.BlockSpec((None, SSD_CONV - 1, nw), lambda bi, g, c: (bi, 0, c_blk0 + g)),
            h0_spec,
            pl.BlockSpec((SSD_CONV, gw), lambda bi, g, c: (0, g)),
            pl.BlockSpec((SSD_CONV, nw), lambda bi, g, c: (0, b_blk0 + g)),
            pl.BlockSpec((SSD_CONV, nw), lambda bi, g, c: (0, c_blk0 + g)),
            pl.BlockSpec((1, gw), lambda bi, g, c: (0, g)),
            pl.BlockSpec((1, nw), lambda bi, g, c: (0, b_blk0 + g)),
            pl.BlockSpec((1, nw), lambda bi, g, c: (0, c_blk0 + g)),
            pl.BlockSpec((None, 1, hw), lambda bi, g, c: (g, 0, 0)),
            pl.BlockSpec((None, hw, 1), lambda bi, g, c: (g, 0, 0)),
            pl.BlockSpec((None, 1, hw), lambda bi, g, c: (g, 0, 0)),
            pl.BlockSpec((None, hw, 1), lambda bi, g, c: (g, 0, 0)),
            pl.BlockSpec((1, gw), lambda bi, g, c: (0, g)),
            pl.BlockSpec((1, gw), lambda bi, g, c: (0, g)),
        ],
        out_specs=[
            pl.BlockSpec((None, q, gw), seq),
            pl.BlockSpec((None, gw, N), lambda bi, g, c: (bi, g, 0)),
        ],
        out_shape=[
            jax.ShapeDtypeStruct((b, L, D_INNER), BF16),
            jax.ShapeDtypeStruct((b, SSD_HEADS * P, N), F32),
        ],
        scratch_shapes=[
            pltpu.VMEM((q + SUBLANES, gw), F32),
            pltpu.VMEM((q + SUBLANES, nw), F32),
            pltpu.VMEM((q + SUBLANES, nw), F32),
            pltpu.VMEM((gw, N), F32),
            pltpu.VMEM((q, R * P), F32),
        ],
        compiler_params=_params("parallel", "parallel", "arbitrary"),
        name="ssd",
    )(z, xbc, xbc, xbc, dt_c, dt_r, conv0, conv0, conv0, h0,
      conv_w, conv_w, conv_w, cbias, cbias, cbias, dbc, dbr, alc, alr, dsk, gn)
    return y, hfin


LOG2E = 1.4426950408889634
SB_LOGIT_SCALE = SB_HEAD_DIM ** -0.5 * LOG2E


def _sb_weights(z2, carry, mask):
    n = z2.shape[1]
    neg_abs = pltpu.bitcast(pltpu.bitcast(z2, jnp.uint32) | jnp.uint32(0x80000000), F32)
    drop = jnp.maximum(z2, 0.0) + jnp.log(1.0 + jnp.exp2(neg_abs)) * LOG2E
    if mask is not None:
        drop = jnp.where(mask, drop, 0.0)
    si = lax.broadcasted_iota(jnp.int32, (2 * n, n), 0)
    ji = lax.broadcasted_iota(jnp.int32, (2 * n, n), 1)
    later = ((si % n) > ji).astype(BF16)
    hi = drop.astype(BF16)
    lo = (drop - hi.astype(F32)).astype(BF16)
    after = jnp.dot(jnp.concatenate([hi, lo], axis=1), later, preferred_element_type=F32) + carry
    attn = jnp.exp2((z2 - drop) - after)
    if mask is not None:
        attn = jnp.where(mask, attn, 0.0)
    return attn, carry + jnp.sum(drop, axis=-1, keepdims=True)


def _sbp_kernel(bias_ref, q_ref, k_ref, v_ref, o_ref, *, tq, nh):
    qi = pl.program_id(2)
    dh = SB_HEAD_DIM
    hp = pl.program_id(1)
    qs = [q_ref[:, h * dh:(h + 1) * dh].astype(BF16) for h in range(nh)]
    bias = jnp.concatenate(
        [jnp.full((tq, 1), bias_ref[hp * nh + h] * LOG2E, F32) for h in range(nh)], axis=0)
    ti = lax.broadcasted_iota(jnp.int32, (nh * tq, tq), 0)
    ji = lax.broadcasted_iota(jnp.int32, (nh * tq, tq), 1)

    def blocks(start, state, mask):
        carry, accs = state
        heads = [slice(h * dh, (h + 1) * dh) for h in range(nh)]
        z = jnp.concatenate([_bdot(qs[h], k_ref[pl.ds(start, tq), heads[h]], _NT) for h in range(nh)],
                            axis=0) * SB_LOGIT_SCALE + bias
        attn, carry = _sb_weights(z, carry, mask)
        accs = tuple(accs[h] + _bdot(attn[h * tq:(h + 1) * tq, :], v_ref[pl.ds(start, tq), heads[h]])
                     for h in range(nh))
        return carry, accs

    init = (jnp.zeros((nh * tq, 1), F32), tuple(jnp.zeros((tq, dh), F32) for _ in range(nh)))
    state = blocks(pl.multiple_of(qi * tq, tq), init, ji < (ti % tq))
    _, accs = lax.fori_loop(
        0, qi, lambda i, st: blocks(pl.multiple_of((qi - 1 - i) * tq, tq), st, None), state)
    for h in range(nh):
        o_ref[:, h * dh:(h + 1) * dh] = accs[h].astype(o_ref.dtype)


def _sb_prompt(q, k, v, bias, *, tq=256, nh=4):
    b, L, _ = q.shape
    w = nh * SB_HEAD_DIM
    kv_spec = pl.BlockSpec((None, L, w), lambda bi, h, i: (bi, 0, h), pipeline_mode=pl.Buffered(1))
    return pl.pallas_call(
        functools.partial(_sbp_kernel, tq=tq, nh=nh),
        grid=(b, SB_HEADS // nh, L // tq),
        in_specs=[
            pl.BlockSpec(memory_space=pltpu.SMEM),
            pl.BlockSpec((None, tq, w), lambda bi, h, i: (bi, i, h)),
            kv_spec, kv_spec,
        ],
        out_specs=pl.BlockSpec((None, tq, w), lambda bi, h, i: (bi, i, h)),
        out_shape=jax.ShapeDtypeStruct((b, L, D_SB), BF16),
        compiler_params=_params("parallel", "parallel", "arbitrary"),
        name="sb_prompt",
    )(bias, q, k, v)


def _sbs_kernel(pt_ref, bias_ref, q_ref, kn_ref, vn_ref, k_hbm, v_hbm, o_ref,
                kbuf, vbuf, sem, acc_ref, carry_ref, *, t, n_steps, pps, layer):
    bi, step = pl.program_id(0), pl.program_id(1)
    H, dh = SB_HEADS, SB_HEAD_DIM
    n_pages = n_steps * pps
    m = H * t
    slot = (bi * n_steps + step) % 2

    def page_copies(seq, stp, slt):
        cps = []
        for i in range(pps):
            page = pt_ref[seq, n_pages - 1 - (stp * pps + i)]
            for h in range(H):
                for hbm, buf in ((k_hbm, kbuf), (v_hbm, vbuf)):
                    cps.append(pltpu.make_async_copy(
                        hbm.at[layer, page, :, h, :], buf.at[slt, i, h], sem.at[slt]))
        return cps

    @pl.when((bi == 0) & (step == 0))
    def _():
        for cp in page_copies(0, 0, 0):
            cp.start()

    wrap = step == n_steps - 1
    nxt_seq = jnp.where(wrap, bi + 1, bi)
    nxt_step = jnp.where(wrap, 0, step + 1)

    @pl.when(nxt_seq < pl.num_programs(0))
    def _():
        for cp in page_copies(nxt_seq, nxt_step, 1 - slot):
            cp.start()

    qs = [q_ref[:, h * dh:(h + 1) * dh].astype(BF16) for h in range(H)]
    bias = bias_ref[...] * LOG2E

    def attend(k_of, v_of, carry, acc, mask):
        z = jnp.concatenate([_bdot(qs[h], k_of(h), _NT) for h in range(H)], axis=0) * SB_LOGIT_SCALE + bias
        attn, carry = _sb_weights(z, carry, mask)
        o = jnp.concatenate([_bdot(attn[h * t:(h + 1) * t, :], v_of(h)) for h in range(H)], axis=0)
        return carry, acc + o

    @pl.when(step == 0)
    def _():
        pad = jnp.zeros((PAGE_SIZE - t, dh), F32)
        new = lambda ref: (lambda h: jnp.concatenate([ref[:, h * dh:(h + 1) * dh], pad], axis=0))
        ri = lax.broadcasted_iota(jnp.int32, (m, PAGE_SIZE), 0)
        ki = lax.broadcasted_iota(jnp.int32, (m, PAGE_SIZE), 1)
        carry, acc = attend(new(kn_ref), new(vn_ref), jnp.zeros((m, 1), F32),
                            jnp.zeros((m, dh), F32), ki < (ri % t))
        carry_ref[...] = carry
        acc_ref[...] = acc

    for cp in page_copies(bi, step, slot):
        cp.wait()
    carry, acc = carry_ref[...], acc_ref[...]
    for i in range(pps):
        head = lambda view: (lambda h: view[slot, i, h])
        carry, acc = attend(head(kbuf), head(vbuf), carry, acc, None)
    carry_ref[...] = carry
    acc_ref[...] = acc

    @pl.when(step == n_steps - 1)
    def _():
        for h in range(H):
            o_ref[:, h * dh:(h + 1) * dh] = acc_ref[h * t:(h + 1) * t, :].astype(o_ref.dtype)


def _sb_sample(q, k, v, bias, cache_k, cache_v, layer, page_table, *, pps=4):
    b, t, _ = q.shape
    H, dh = SB_HEADS, SB_HEAD_DIM
    n_pages = page_table.shape[1]
    pps = min(pps, n_pages)
    assert n_pages % pps == 0
    n_steps = n_pages // pps
    m = H * t
    bias_rows = jnp.repeat(bias, t).reshape(m, 1)
    tok = lambda bi, s, pt: (bi, 0, 0)
    stage = pltpu.VMEM((2, pps, H, PAGE_SIZE, dh), F32)
    return pl.pallas_call(
        functools.partial(_sbs_kernel, t=t, n_steps=n_steps, pps=pps, layer=layer),
        grid_spec=pltpu.PrefetchScalarGridSpec(
            num_scalar_prefetch=1,
            grid=(b, n_steps),
            in_specs=[
                pl.BlockSpec((m, 1), lambda bi, s, pt: (0, 0)),
                pl.BlockSpec((None, t, D_SB), tok),
                pl.BlockSpec((None, t, D_SB), tok),
                pl.BlockSpec((None, t, D_SB), tok),
                pl.BlockSpec(memory_space=pl.ANY),
                pl.BlockSpec(memory_space=pl.ANY),
            ],
            out_specs=pl.BlockSpec((None, t, D_SB), tok),
            scratch_shapes=[
                stage, stage,
                pltpu.SemaphoreType.DMA((2,)),
                pltpu.VMEM((m, dh), F32),
                pltpu.VMEM((m, 1), F32),
            ],
        ),
        out_shape=jax.ShapeDtypeStruct((b, t, D_SB), BF16),
        compiler_params=_params("arbitrary", "arbitrary"),
        name="sb_sample",
    )(page_table, bias_rows, q, k, v, cache_k, cache_v)


def kernel(x_prompt, x_sample, mem_prompt, state_ssm, state_conv, cache_k, cache_v, cache_mem_k, cache_mem_v,
           page_table, g_ffn1, w_ffn1_in, w_ffn1_out, g_mix, g_mem, w_mem_kv, w_in_ssd, conv_w, conv_b,
           dt_bias, a_log, d_skip, g_ssd_norm, w_out_ssd, w_in_sb, sb_bias, w_out_sb, g_ffn2, w_ffn2_in,
           w_ffn2_out, g_final):
    bp, lp, d = x_prompt.shape
    bs, ls, _ = x_sample.shape
    depth = g_ffn1.shape[0]
    H, P, N = SSD_HEADS, SSD_HEAD_DIM, SSD_STATE
    groups = ((bp, lp), (bs, ls))
    xs = [x_prompt.reshape(bp * lp, d), x_sample.reshape(bs * ls, d)]
    mem2d = mem_prompt.reshape(bp * N_MEM, d)

    p_ssm, p_conv, p_k, p_v, p_mk, p_mv = [], [], [], [], [], []
    s_ssm, s_conv, s_k, s_v = [], [], [], []
    for i in range(depth):
        j = i // 2
        last = i == depth - 1
        xs = [_ffn(x, g_ffn1[i], w_ffn1_in, w_ffn1_out, i) for x in xs]

        mk = _proj(mem2d, g_mem[i], w_mem_kv, i, 0, D_MEM)
        mv = _proj(mem2d, g_mem[i], w_mem_kv, i, D_MEM, D_MEM)
        p_mk.append(mk.reshape(bp, N_MEM, MEM_HEADS, MEM_HEAD_DIM))
        p_mv.append(mv.reshape(bp, N_MEM, MEM_HEADS, MEM_HEAD_DIM))
        mk3, mv3 = mk.reshape(bp, N_MEM, D_MEM), mv.reshape(bp, N_MEM, D_MEM)

        new_xs = []
        for gi, (x, (b, L)) in enumerate(zip(xs, groups)):
            prompt = gi == 0
            g = g_mix[i]
            if i % 2 == 0:
                assert L >= SSD_CONV - 1
                dt0 = D_INNER + CONV_DIM
                z = _proj(x, g, w_in_ssd, j, 0, D_INNER)
                xbc = _proj(x, g, w_in_ssd, j, D_INNER, CONV_DIM)
                dt = _proj(x, g, w_in_ssd[j:j + 1, :, dt0:dt0 + H], 0, 0, H)
                qm = _proj(x, g, w_in_ssd[j:j + 1, :, dt0 + H:], 0, 0, D_MEM, out_dtype=BF16)
                xbc3 = xbc.reshape(b, L, CONV_DIM)
                if prompt:
                    conv0, h0, q, ng = None, None, SSD_CHUNK, 1
                else:
                    conv0, h0, q, ng = state_conv[j], state_ssm[j].reshape(b, H * P, N), L, SSD_GROUPS
                y, hfin = _ssd(z.reshape(b, L, D_INNER), xbc3, dt.reshape(b, L, H), conv0, h0,
                               conv_w[j], conv_b[j], dt_bias[j], a_log[j], d_skip[j], g_ssd_norm[j],
                               q=q, ng=ng)
                (p_conv if prompt else s_conv).append(xbc3[:, L - (SSD_CONV - 1):, :])
                (p_ssm if prompt else s_ssm).append(hfin.reshape(b, H, P, N))
                a1, w_out = y.reshape(b * L, D_INNER), w_out_ssd
            else:
                qh = _proj(x, g, w_in_sb, j, 0, D_SB, out_dtype=BF16)
                kh = _proj(x, g, w_in_sb, j, D_SB, D_SB)
                vh = _proj(x, g, w_in_sb, j, 2 * D_SB, D_SB)
                qm = _proj(x, g, w_in_sb, j, 3 * D_SB, D_MEM, out_dtype=BF16)
                q3, k3, v3 = (a.reshape(b, L, D_SB) for a in (qh, kh, vh))
                if prompt:
                    o = _sb_prompt(q3, k3, v3, sb_bias[j])
                else:
                    o = _sb_sample(q3, k3, v3, sb_bias[j], cache_k, cache_v, j, page_table)
                (p_k if prompt else s_k).append(kh.reshape(b, L, SB_HEADS, SB_HEAD_DIM))
                (p_v if prompt else s_v).append(vh.reshape(b, L, SB_HEADS, SB_HEAD_DIM))
                a1, w_out = o.reshape(b * L, D_SB), w_out_sb
            qm3 = qm.reshape(b, L, D_MEM)
            o_mem = _memattn(qm3, mk3, mv3) if prompt else _memattn_cache(qm3, cache_mem_k, cache_mem_v, i)
            x = _outproj(x, a1, o_mem.reshape(b * L, D_MEM), w_out, j)
            x = _ffn(x, g_ffn2[i], w_ffn2_in, w_ffn2_out, i, g_final if last else None)
            new_xs.append(x)
        xs = new_xs

    return (xs[0].reshape(bp, lp, d), xs[1].reshape(bs, ls, d),
            jnp.stack(p_ssm), jnp.stack(p_conv), jnp.stack(p_k), jnp.stack(p_v),
            jnp.stack(p_mk), jnp.stack(p_mv), jnp.stack(s_ssm), jnp.stack(s_conv), jnp.stack(s_k), jnp.stack(s_v))
```

```python
import functools

import jax
import jax.numpy as jnp
from jax import lax
from jax.experimental import pallas as pl
from jax.experimental.pallas import tpu as pltpu

F32 = jnp.float32
BF16 = jnp.bfloat16

D_MODEL = 2048
N_MEM = 256
MEM_HEADS = 4
MEM_HEAD_DIM = D_MODEL // MEM_HEADS
D_MEM = MEM_HEADS * MEM_HEAD_DIM
D_INNER = 2 * D_MODEL
SSD_HEAD_DIM = 64
SSD_HEADS = D_INNER // SSD_HEAD_DIM
SSD_GROUPS = 8
SSD_HPG = SSD_HEADS // SSD_GROUPS
SSD_STATE = 128
SSD_CONV = 4
SSD_CHUNK = 256
GROUP_DIM = D_INNER // SSD_GROUPS
GROUP_CONV = GROUP_DIM + 2 * SSD_STATE
CONV_DIM = D_INNER + 2 * SSD_GROUPS * SSD_STATE
SB_HEAD_DIM = 128
SB_HEADS = D_MODEL // SB_HEAD_DIM
D_SB = SB_HEADS * SB_HEAD_DIM
PAGE_SIZE = 128
FFN_RES = 0.5
EPS = 1e-6

VMEM_LIMIT_BYTES = 56 * 1024 * 1024
SUBLANES = 8

_NT = (((1,), (1,)), ((), ()))
_TN = (((0,), (0,)), ((), ()))


def _params(*semantics):
    return pltpu.CompilerParams(dimension_semantics=semantics, vmem_limit_bytes=VMEM_LIMIT_BYTES)


def _rms(x):
    return x * lax.rsqrt(jnp.mean(x * x, axis=-1, keepdims=True) + EPS)


def _silu(x):
    return x * jax.nn.sigmoid(x)


def _softplus(x):
    return jnp.maximum(x, 0.0) + jnp.log1p(jnp.exp(-jnp.abs(x)))


def _bdot(a, b, dims=None):
    a = a.astype(BF16)
    b = b.astype(BF16)
    if dims is None:
        return jnp.dot(a, b, preferred_element_type=F32)
    return lax.dot_general(a, b, dims, preferred_element_type=F32)


def _split3(x):
    x1 = x.astype(BF16)
    r1 = x - x1.astype(F32)
    x2 = r1.astype(BF16)
    return x1, x2, (r1 - x2.astype(F32)).astype(BF16)


def _dot01(a, b, *, exact):
    if exact == "a":
        return sum(jnp.dot(a.astype(BF16), p, preferred_element_type=F32) for p in _split3(b))
    return sum(jnp.dot(p, b.astype(BF16), preferred_element_type=F32) for p in _split3(a))


def _ffn_kernel(x_ref, g_ref, wg_ref, wu_ref, wo_ref, gf_ref, o_ref, h_ref, *, n_ff, final_norm):
    j = pl.program_id(1)

    @pl.when(j == 0)
    def _():
        h_ref[...] = (_rms(x_ref[...]) * g_ref[...]).astype(BF16)
        o_ref[...] = jnp.zeros_like(o_ref)

    h = h_ref[...]
    gate = _bdot(h, wg_ref[...])
    up = _bdot(h, wu_ref[...])
    o_ref[...] += _bdot(_silu(gate) * up, wo_ref[...])

    @pl.when(j == n_ff - 1)
    def _():
        y = x_ref[...] + FFN_RES * o_ref[...]
        if final_norm:
            y = _rms(y) * gf_ref[...]
        o_ref[...] = y


def _ffn(x, g, w_in, w_out, layer, g_final=None, *, tm=1024, tf=256):
    t, d = x.shape
    d_ff = w_out.shape[1]
    tm = min(tm, t)
    n_ff = d_ff // tf
    final_norm = g_final is not None
    gf = g_final if final_norm else g
    return pl.pallas_call(
        functools.partial(_ffn_kernel, n_ff=n_ff, final_norm=final_norm),
        grid=(t // tm, n_ff),
        in_specs=[
            pl.BlockSpec((tm, d), lambda i, j: (i, 0), pipeline_mode=pl.Buffered(1)),
            pl.BlockSpec((1, d), lambda i, j: (0, 0)),
            pl.BlockSpec((None, d, tf), lambda i, j: (layer, 0, j)),
            pl.BlockSpec((None, d, tf), lambda i, j: (layer, 0, n_ff + j)),
            pl.BlockSpec((None, tf, d), lambda i, j: (layer, j, 0)),
            pl.BlockSpec((1, d), lambda i, j: (0, 0)),
        ],
        out_specs=pl.BlockSpec((tm, d), lambda i, j: (i, 0)),
        out_shape=jax.ShapeDtypeStruct((t, d), F32),
        scratch_shapes=[pltpu.VMEM((tm, d), BF16)],
        compiler_params=_params("parallel", "arbitrary"),
        name="ffn",
    )(x, g.reshape(1, d), w_in, w_in, w_out, gf.reshape(1, d))


def _proj_kernel(x_ref, g_ref, w_ref, o_ref, h_ref):
    @pl.when(pl.program_id(1) == 0)
    def _():
        h_ref[...] = (_rms(x_ref[...]) * g_ref[...]).astype(BF16)

    o_ref[...] = _bdot(h_ref[...], w_ref[...]).astype(o_ref.dtype)


def _proj(x, g, w, layer, col0, n, *, out_dtype=F32, tm=1024, tn=512):
    t, d = x.shape
    tm = min(tm, t)
    tn = min(tn, n)
    assert n % tn == 0 and col0 % tn == 0
    cb = col0 // tn
    return pl.pallas_call(
        _proj_kernel,
        grid=(t // tm, n // tn),
        in_specs=[
            pl.BlockSpec((tm, d), lambda i, j: (i, 0)),
            pl.BlockSpec((1, d), lambda i, j: (0, 0)),
            pl.BlockSpec((None, d, tn), lambda i, j: (layer, 0, cb + j)),
        ],
        out_specs=pl.BlockSpec((tm, tn), lambda i, j: (i, j)),
        out_shape=jax.ShapeDtypeStruct((t, n), out_dtype),
        scratch_shapes=[pltpu.VMEM((tm, d), BF16)],
        compiler_params=_params("parallel", "arbitrary"),
        name="proj",
    )(x, g.reshape(1, d), w)


def _outproj_kernel(x_ref, a1_ref, a2_ref, w1_ref, w2_ref, o_ref):
    o_ref[...] = x_ref[...] + _bdot(a1_ref[...], w1_ref[...]) + _bdot(a2_ref[...], w2_ref[...])


def _outproj(x, a1, a2, w, layer, *, tm=1024, tn=256):
    t, d = x.shape
    k1 = a1.shape[1]
    k2 = a2.shape[1]
    assert w.shape[1:] == (k1 + k2, d) and k1 % k2 == 0
    tm = min(tm, t)
    return pl.pallas_call(
        _outproj_kernel,
        grid=(t // tm, d // tn),
        in_specs=[
            pl.BlockSpec((tm, tn), lambda i, j: (i, j)),
            pl.BlockSpec((tm, k1), lambda i, j: (i, 0)),
            pl.BlockSpec((tm, k2), lambda i, j: (i, 0)),
            pl.BlockSpec((None, k1, tn), lambda i, j: (layer, 0, j)),
            pl.BlockSpec((None, k2, tn), lambda i, j: (layer, k1 // k2, j)),
        ],
        out_specs=pl.BlockSpec((tm, tn), lambda i, j: (i, j)),
        out_shape=jax.ShapeDtypeStruct((t, d), F32),
        compiler_params=_params("parallel", "arbitrary"),
        name="outproj",
    )(x, a1, a2, w, w)


def _mem_head(q, k, v):
    s = _bdot(q, k, _NT) * (MEM_HEAD_DIM ** -0.5)
    p = jnp.exp(s - jnp.max(s, axis=-1, keepdims=True))
    p = p / jnp.sum(p, axis=-1, keepdims=True)
    return _bdot(p, v)


def _memattn_kernel(q_ref, k_ref, v_ref, o_ref):
    for h in range(MEM_HEADS):
        sl = slice(h * MEM_HEAD_DIM, (h + 1) * MEM_HEAD_DIM)
        o_ref[:, sl] = _mem_head(q_ref[:, sl], k_ref[:, sl], v_ref[:, sl]).astype(o_ref.dtype)


def _memattn(q, mk, mv, *, tq=512):
    b, t, _ = q.shape
    tq = min(tq, t)
    kv_spec = pl.BlockSpec((None, N_MEM, D_MEM), lambda i, j: (i, 0, 0))
    return pl.pallas_call(
        _memattn_kernel,
        grid=(b, t // tq),
        in_specs=[pl.BlockSpec((None, tq, D_MEM), lambda i, j: (i, j, 0)), kv_spec, kv_spec],
        out_specs=pl.BlockSpec((None, tq, D_MEM), lambda i, j: (i, j, 0)),
        out_shape=jax.ShapeDtypeStruct((b, t, D_MEM), BF16),
        compiler_params=_params("parallel", "arbitrary"),
        name="memattn",
    )(q, mk, mv)


def _memattn_cache_kernel(q_ref, k_hbm, v_hbm, o_ref, kbuf, vbuf, sem, *, bb, layer):
    i = pl.program_id(0)
    slot = i % 2

    def head_copies(stp, slt):
        cps = []
        for bi in range(bb):
            for h in range(MEM_HEADS):
                for hbm, buf in ((k_hbm, kbuf), (v_hbm, vbuf)):
                    cps.append(pltpu.make_async_copy(
                        hbm.at[layer, stp * bb + bi, :, h, :], buf.at[slt, bi, h], sem.at[slt]))
        return cps

    @pl.when(i == 0)
    def _():
        for cp in head_copies(0, 0):
            cp.start()

    @pl.when(i + 1 < pl.num_programs(0))
    def _():
        for cp in head_copies(i + 1, 1 - slot):
            cp.start()

    for cp in head_copies(i, slot):
        cp.wait()
    for bi in range(bb):
        for h in range(MEM_HEADS):
            sl = slice(h * MEM_HEAD_DIM, (h + 1) * MEM_HEAD_DIM)
            o_ref[bi, :, sl] = _mem_head(q_ref[bi, :, sl], kbuf[slot, bi, h], vbuf[slot, bi, h]).astype(o_ref.dtype)


def _memattn_cache(q, cache_k, cache_v, layer, *, bb=2):
    b, t, _ = q.shape
    assert b % bb == 0
    stage = pltpu.VMEM((2, bb, MEM_HEADS, N_MEM, MEM_HEAD_DIM), F32)
    return pl.pallas_call(
        functools.partial(_memattn_cache_kernel, bb=bb, layer=layer),
        grid=(b // bb,),
        in_specs=[pl.BlockSpec((bb, t, D_MEM), lambda i: (i, 0, 0)),
                  pl.BlockSpec(memory_space=pl.ANY), pl.BlockSpec(memory_space=pl.ANY)],
        out_specs=pl.BlockSpec((bb, t, D_MEM), lambda i: (i, 0, 0)),
        out_shape=jax.ShapeDtypeStruct((b, t, D_MEM), BF16),
        scratch_shapes=[stage, stage, pltpu.SemaphoreType.DMA((2,))],
        compiler_params=_params("arbitrary"),
        name="memattn_cache",
    )(q, cache_k, cache_v)


def _ssd_kernel(z_ref, x_ref, b_ref, c_ref, dtc_ref, dtr_ref,
                cx0_ref, cb0_ref, cc0_ref, h0_ref,
                wx_ref, wb_ref, wc_ref, bx_ref, bb_ref, bc_ref,
                dbc_ref, dbr_ref, alc_ref, alr_ref, dsk_ref, gn_ref,
                y_ref, hfin_ref,
                xp_ref, bp_ref, cp_ref, h_ref, yd_ref, *, q, ng, n_chunks, zero_init):
    c = pl.program_id(2)
    R, P, N = SSD_HPG, SSD_HEAD_DIM, SSD_STATE
    pad = SUBLANES

    @pl.when(c == 0)
    def _():
        for ref, init in ((xp_ref, cx0_ref), (bp_ref, cb0_ref), (cp_ref, cc0_ref)):
            ref[0:pad, :] = jnp.zeros((pad, ref.shape[1]), F32)
            if not zero_init:
                ref[pad - (SSD_CONV - 1):pad, :] = init[...]
        if zero_init:
            h_ref[...] = jnp.zeros_like(h_ref)
        else:
            h_ref[...] = h0_ref[...]

    def conv(raw_ref, p_ref, w_ref, bias_ref):
        p_ref[pad:pad + q, :] = raw_ref[...]
        acc = bias_ref[...] + w_ref[SSD_CONV - 1:SSD_CONV, :] * p_ref[pad:pad + q, :]
        for k in range(SSD_CONV - 1):
            off = pad - (SSD_CONV - 1) + k
            acc = acc + w_ref[k:k + 1, :] * p_ref[off:off + q, :]
        p_ref[0:pad, :] = p_ref[q:q + pad, :]
        return _silu(acc)

    xc = conv(x_ref, xp_ref, wx_ref, bx_ref)
    bm = conv(b_ref, bp_ref, wb_ref, bb_ref)
    cm = conv(c_ref, cp_ref, wc_ref, bc_ref)

    dt_c = _softplus(dtc_ref[...] + dbc_ref[...])
    dt_r = _softplus(dtr_ref[...] + dbr_ref[...])
    da_c = dt_c * (-jnp.exp(alc_ref[...]))
    da_r = dt_r * (-jnp.exp(alr_ref[...]))
    ti = lax.broadcasted_iota(jnp.int32, (q, q), 0)
    si = lax.broadcasted_iota(jnp.int32, (q, q), 1)
    tril = si <= ti
    cs_c = _dot01(tril, da_c, exact="a")
    cs_r = _dot01(da_r, ti <= si, exact="b")

    hi = lax.broadcasted_iota(jnp.int32, (R, R * P), 0)
    ci = lax.broadcasted_iota(jnp.int32, (R, R * P), 1)
    expand = ((ci >= hi * P) & (ci < (hi + 1) * P)).astype(F32)

    for gi in range(ng):
        gs = slice(gi * R * P, (gi + 1) * R * P)
        ns = slice(gi * N, (gi + 1) * N)
        hs = slice(gi * R, (gi + 1) * R)
        x_g, b_g, c_g = xc[:, gs], bm[:, ns], cm[:, ns]
        csx = _dot01(cs_c[:, hs], expand, exact="b")
        xdt = x_g * _dot01(dt_c[:, hs], expand, exact="b")
        cb = _bdot(c_g, b_g, _NT)

        if q <= SUBLANES:
            trow = lax.broadcasted_iota(jnp.int32, (q, R * P), 0)
            yd = jnp.zeros((q, R * P), F32)
            for s in range(q):
                w = jnp.exp(jnp.where(trow >= s, csx - csx[s:s + 1, :], -jnp.inf))
                yd = yd + (cb[:, s:s + 1] * w) * xdt[s:s + 1, :]
        else:
            for r in range(R):
                col = cs_c[:, gi * R + r:gi * R + r + 1]
                row = cs_r[gi * R + r:gi * R + r + 1, :]
                m = cb * jnp.exp(jnp.where(tril, col - row, -jnp.inf))
                yd_ref[:, r * P:(r + 1) * P] = _bdot(m, xdt[:, r * P:(r + 1) * P])
            yd = yd_ref[...]

        h_old = h_ref[gs, :]
        y_off = _bdot(c_g, h_old, _NT) * jnp.exp(csx)
        x_end = xdt * jnp.exp(csx[q - 1:q, :] - csx)
        st = _bdot(x_end, b_g, _TN)
        for r in range(R):
            rs = slice(gi * R * P + r * P, gi * R * P + (r + 1) * P)
            dec = jnp.exp(cs_r[gi * R + r:gi * R + r + 1, q - 1:q])
            h_ref[rs, :] = h_ref[rs, :] * dec + st[r * P:(r + 1) * P, :]

        y = yd + y_off + dsk_ref[:, gs] * x_g
        y = y * _silu(z_ref[:, gs])
        y_ref[:, gs] = (_rms(y) * gn_ref[:, gs]).astype(y_ref.dtype)

    @pl.when(c == n_chunks - 1)
    def _():
        hfin_ref[...] = h_ref[...]


def _ssd(z, xbc, dt, conv0, h0, conv_w, conv_b, dt_bias, a_log, d_skip, g_norm, *, q, ng):
    b, L, _ = z.shape
    G, R, P, N = SSD_GROUPS, SSD_HPG, SSD_HEAD_DIM, SSD_STATE
    zero_init = h0 is None
    if zero_init:
        conv0 = jnp.zeros((b, SSD_CONV - 1, CONV_DIM), F32)
        h0 = jnp.zeros((b, SUBLANES, N), F32)
    ngrp = G // ng
    n_chunks = L // q
    gw, nw, hw = ng * R * P, ng * N, ng * R
    b_blk0 = D_INNER // nw
    c_blk0 = (D_INNER + G * N) // nw

    dt_c = dt.reshape(b, L, ngrp, hw).transpose(0, 2, 1, 3)
    dt_r = dt_c.transpose(0, 1, 3, 2)
    dbc = dt_bias.reshape(ngrp, 1, hw)
    dbr = dt_bias.reshape(ngrp, hw, 1)
    alc = a_log.reshape(ngrp, 1, hw)
    alr = a_log.reshape(ngrp, hw, 1)
    dsk = jnp.repeat(d_skip, P).reshape(1, D_INNER)
    gn = g_norm.reshape(1, D_INNER)
    cbias = conv_b.reshape(1, CONV_DIM)

    seq = lambda bi, g, c: (bi, c, g)
    kern = functools.partial(_ssd_kernel, q=q, ng=ng, n_chunks=n_chunks, zero_init=zero_init)
    h0_spec = (pl.BlockSpec((None, SUBLANES, N), lambda bi, g, c: (bi, 0, 0)) if zero_init
               else pl.BlockSpec((None, gw, N), lambda bi, g, c: (bi, g, 0)))
    y, hfin = pl.pallas_call(
        kern,
        grid=(b, ngrp, n_chunks),
        in_specs=[
            pl.BlockSpec((None, q, gw), seq),
            pl.BlockSpec((None, q, gw), seq),
            pl.BlockSpec((None, q, nw), lambda bi, g, c: (bi, c, b_blk0 + g)),
            pl.BlockSpec((None, q, nw), lambda bi, g, c: (bi, c, c_blk0 + g)),
            pl.BlockSpec((None, None, q, hw), lambda bi, g, c: (bi, g, c, 0)),
            pl.BlockSpec((None, None, hw, q), lambda bi, g, c: (bi, g, 0, c)),
            pl.BlockSpec((None, SSD_CONV - 1, gw), lambda bi, g, c: (bi, 0, g)),
            pl.BlockSpec((None, SSD_CONV - 1, nw), lambda bi, g, c: (bi, 0, b_blk0 + g)),
            pl.BlockSpec((None, SSD_CONV - 1, nw), lambda bi, g, c: (bi, 0, c_blk0 + g)),
            h0_spec,
            pl.BlockSpec((SSD_CONV, gw), lambda bi, g, c: (0, g)),
            pl.BlockSpec((SSD_CONV, nw), lambda bi, g, c: (0, b_blk0 + g)),
            pl.BlockSpec((SSD_CONV, nw), lambda bi, g, c: (0, c_blk0 + g)),
            pl.BlockSpec((1, gw), lambda bi, g, c: (0, g)),
            pl.BlockSpec((1, nw), lambda bi, g, c: (0, b_blk0 + g)),
            pl.BlockSpec((1, nw), lambda bi, g, c: (0, c_blk0 + g)),
            pl.BlockSpec((None, 1, hw), lambda bi, g, c: (g, 0, 0)),
            pl.BlockSpec((None, hw, 1), lambda bi, g, c: (g, 0, 0)),
            pl.BlockSpec((None, 1, hw), lambda bi, g, c: (g, 0, 0)),
            pl.BlockSpec((None, hw, 1), lambda bi, g, c: (g, 0, 0)),
            pl.BlockSpec((1, gw), lambda bi, g, c: (0, g)),
            pl.BlockSpec((1, gw), lambda bi, g, c: (0, g)),
        ],
        out_specs=[
            pl.BlockSpec((None, q, gw), seq),
            pl.BlockSpec((None, gw, N), lambda bi, g, c: (bi, g, 0)),
        ],
        out_shape=[
            jax.ShapeDtypeStruct((b, L, D_INNER), BF16),
            jax.ShapeDtypeStruct((b, SSD_HEADS * P, N), F32),
        ],
        scratch_shapes=[
            pltpu.VMEM((q + SUBLANES, gw), F32),
            pltpu.VMEM((q + SUBLANES, nw), F32),
            pltpu.VMEM((q + SUBLANES, nw), F32),
            pltpu.VMEM((gw, N), F32),
            pltpu.VMEM((q, R * P), F32),
        ],
        compiler_params=_params("parallel", "parallel", "arbitrary"),
        name="ssd",
    )(z, xbc, xbc, xbc, dt_c, dt_r, conv0, conv0, conv0, h0,
      conv_w, conv_w, conv_w, cbias, cbias, cbias, dbc, dbr, alc, alr, dsk, gn)
    return y, hfin


LOG2E = 1.4426950408889634
SB_LOGIT_SCALE = SB_HEAD_DIM ** -0.5 * LOG2E


def _sb_weights(z2, carry, mask):
    n = z2.shape[1]
    neg_abs = pltpu.bitcast(pltpu.bitcast(z2, jnp.uint32) | jnp.uint32(0x80000000), F32)
    drop = jnp.maximum(z2, 0.0) + jnp.log(1.0 + jnp.exp2(neg_abs)) * LOG2E
    if mask is not None:
        drop = jnp.where(mask, drop, 0.0)
    si = lax.broadcasted_iota(jnp.int32, (2 * n, n), 0)
    ji = lax.broadcasted_iota(jnp.int32, (2 * n, n), 1)
    later = ((si % n) > ji).astype(BF16)
    hi = drop.astype(BF16)
    lo = (drop - hi.astype(F32)).astype(BF16)
    after = jnp.dot(jnp.concatenate([hi, lo], axis=1), later, preferred_element_type=F32) + carry
    attn = jnp.exp2((z2 - drop) - after)
    if mask is not None:
        attn = jnp.where(mask, attn, 0.0)
    return attn, carry + jnp.sum(drop, axis=-1, keepdims=True)


def _sbp_kernel(bias_ref, q_ref, k_ref, v_ref, o_ref, *, tq, nh):
    qi = pl.program_id(2)
    dh = SB_HEAD_DIM
    hp = pl.program_id(1)
    qs = [q_ref[:, h * dh:(h + 1) * dh].astype(BF16) for h in range(nh)]
    bias = jnp.concatenate(
        [jnp.full((tq, 1), bias_ref[hp * nh + h] * LOG2E, F32) for h in range(nh)], axis=0)
    ti = lax.broadcasted_iota(jnp.int32, (nh * tq, tq), 0)
    ji = lax.broadcasted_iota(jnp.int32, (nh * tq, tq), 1)

    def blocks(start, state, mask):
        carry, accs = state
        heads = [slice(h * dh, (h + 1) * dh) for h in range(nh)]
        z = jnp.concatenate([_bdot(qs[h], k_ref[pl.ds(start, tq), heads[h]], _NT) for h in range(nh)],
                            axis=0) * SB_LOGIT_SCALE + bias
        attn, carry = _sb_weights(z, carry, mask)
        accs = tuple(accs[h] + _bdot(attn[h * tq:(h + 1) * tq, :], v_ref[pl.ds(start, tq), heads[h]])
                     for h in range(nh))
        return carry, accs

    init = (jnp.zeros((nh * tq, 1), F32), tuple(jnp.zeros((tq, dh), F32) for _ in range(nh)))
    block_at = lambda kb, st: blocks(pl.multiple_of(kb * tq, tq), st, None)
    state = blocks(pl.multiple_of(qi * tq, tq), init, ji < (ti % tq))
    state = lax.cond(qi % 2 == 1, lambda st: block_at(qi - 1, st), lambda st: st, state)
    top = qi - qi % 2
    _, accs = lax.fori_loop(
        0, qi // 2, lambda i, st: block_at(top - 2 - 2 * i, block_at(top - 1 - 2 * i, st)), state)
    for h in range(nh):
        o_ref[:, h * dh:(h + 1) * dh] = accs[h].astype(o_ref.dtype)


def _sb_prompt(q, k, v, bias, *, tq=256, nh=8):
    b, L, _ = q.shape
    w = nh * SB_HEAD_DIM
    kv_spec = pl.BlockSpec((None, L, w), lambda bi, h, i: (bi, 0, h), pipeline_mode=pl.Buffered(1))
    return pl.pallas_call(
        functools.partial(_sbp_kernel, tq=tq, nh=nh),
        grid=(b, SB_HEADS // nh, L // tq),
        in_specs=[
            pl.BlockSpec(memory_space=pltpu.SMEM),
            pl.BlockSpec((None, tq, w), lambda bi, h, i: (bi, i, h)),
            kv_spec, kv_spec,
        ],
        out_specs=pl.BlockSpec((None, tq, w), lambda bi, h, i: (bi, i, h)),
        out_shape=jax.ShapeDtypeStruct((b, L, D_SB), BF16),
        compiler_params=_params("parallel", "parallel", "arbitrary"),
        name="sb_prompt",
    )(bias, q, k, v)


def _sbs_kernel(pt_ref, bias_ref, q_ref, kn_ref, vn_ref, k_hbm, v_hbm, o_ref,
                kbuf, vbuf, sem, acc_ref, carry_ref, *, t, n_steps, pps, layer):
    bi, step = pl.program_id(0), pl.program_id(1)
    H, dh = SB_HEADS, SB_HEAD_DIM
    n_pages = n_steps * pps
    m = H * t
    slot = (bi * n_steps + step) % 2

    def page_copies(seq, stp, slt):
        cps = []
        for i in range(pps):
            page = pt_ref[seq, n_pages - 1 - (stp * pps + i)]
            for h in range(H):
                for hbm, buf in ((k_hbm, kbuf), (v_hbm, vbuf)):
                    cps.append(pltpu.make_async_copy(
                        hbm.at[layer, page, :, h, :], buf.at[slt, i, h], sem.at[slt]))
        return cps

    @pl.when((bi == 0) & (step == 0))
    def _():
        for cp in page_copies(0, 0, 0):
            cp.start()

    wrap = step == n_steps - 1
    nxt_seq = jnp.where(wrap, bi + 1, bi)
    nxt_step = jnp.where(wrap, 0, step + 1)

    @pl.when(nxt_seq < pl.num_programs(0))
    def _():
        for cp in page_copies(nxt_seq, nxt_step, 1 - slot):
            cp.start()

    qs = [q_ref[:, h * dh:(h + 1) * dh].astype(BF16) for h in range(H)]
    bias = bias_ref[...] * LOG2E

    def attend(k_of, v_of, carry, acc, mask):
        z = jnp.concatenate([_bdot(qs[h], k_of(h), _NT) for h in range(H)], axis=0) * SB_LOGIT_SCALE + bias
        attn, carry = _sb_weights(z, carry, mask)
        o = jnp.concatenate([_bdot(attn[h * t:(h + 1) * t, :], v_of(h)) for h in range(H)], axis=0)
        return carry, acc + o

    @pl.when(step == 0)
    def _():
        pad = jnp.zeros((PAGE_SIZE - t, dh), F32)
        new = lambda ref: (lambda h: jnp.concatenate([ref[:, h * dh:(h + 1) * dh], pad], axis=0))
        ri = lax.broadcasted_iota(jnp.int32, (m, PAGE_SIZE), 0)
        ki = lax.broadcasted_iota(jnp.int32, (m, PAGE_SIZE), 1)
        carry, acc = attend(new(kn_ref), new(vn_ref), jnp.zeros((m, 1), F32),
                            jnp.zeros((m, dh), F32), ki < (ri % t))
        carry_ref[...] = carry
        acc_ref[...] = acc

    for cp in page_copies(bi, step, slot):
        cp.wait()
    carry, acc = carry_ref[...], acc_ref[...]
    for i in range(pps):
        head = lambda view: (lambda h: view[slot, i, h])
        carry, acc = attend(head(kbuf), head(vbuf), carry, acc, None)
    carry_ref[...] = carry
    acc_ref[...] = acc

    @pl.when(step == n_steps - 1)
    def _():
        for h in range(H):
            o_ref[:, h * dh:(h + 1) * dh] = acc_ref[h * t:(h + 1) * t, :].astype(o_ref.dtype)


def _sb_sample(q, k, v, bias, cache_k, cache_v, layer, page_table, *, pps=4):
    b, t, _ = q.shape
    H, dh = SB_HEADS, SB_HEAD_DIM
    n_pages = page_table.shape[1]
    pps = min(pps, n_pages)
    assert n_pages % pps == 0
    n_steps = n_pages // pps
    m = H * t
    bias_rows = jnp.repeat(bias, t).reshape(m, 1)
    tok = lambda bi, s, pt: (bi, 0, 0)
    stage = pltpu.VMEM((2, pps, H, PAGE_SIZE, dh), F32)
    return pl.pallas_call(
        functools.partial(_sbs_kernel, t=t, n_steps=n_steps, pps=pps, layer=layer),
        grid_spec=pltpu.PrefetchScalarGridSpec(
            num_scalar_prefetch=1,
            grid=(b, n_steps),
            in_specs=[
                pl.BlockSpec((m, 1), lambda bi, s, pt: (0, 0)),
                pl.BlockSpec((None, t, D_SB), tok),
                pl.BlockSpec((None, t, D_SB), tok),
                pl.BlockSpec((None, t, D_SB), tok),
                pl.BlockSpec(memory_space=pl.ANY),
                pl.BlockSpec(memory_space=pl.ANY),
            ],
            out_specs=pl.BlockSpec((None, t, D_SB), tok),
            scratch_shapes=[
                stage, stage,
                pltpu.SemaphoreType.DMA((2,)),
                pltpu.VMEM((m, dh), F32),
                pltpu.VMEM((m, 1), F32),
            ],
        ),
        out_shape=jax.ShapeDtypeStruct((b, t, D_SB), BF16),
        compiler_params=_params("arbitrary", "arbitrary"),
        name="sb_sample",
    )(page_table, bias_rows, q, k, v, cache_k, cache_v)


def kernel(x_prompt, x_sample, mem_prompt, state_ssm, state_conv, cache_k, cache_v, cache_mem_k, cache_mem_v,
           page_table, g_ffn1, w_ffn1_in, w_ffn1_out, g_mix, g_mem, w_mem_kv, w_in_ssd, conv_w, conv_b,
           dt_bias, a_log, d_skip, g_ssd_norm, w_out_ssd, w_in_sb, sb_bias, w_out_sb, g_ffn2, w_ffn2_in,
           w_ffn2_out, g_final):
    bp, lp, d = x_prompt.shape
    bs, ls, _ = x_sample.shape
    depth = g_ffn1.shape[0]
    H, P, N = SSD_HEADS, SSD_HEAD_DIM, SSD_STATE
    groups = ((bp, lp), (bs, ls))
    xs = [x_prompt.reshape(bp * lp, d), x_sample.reshape(bs * ls, d)]
    mem2d = mem_prompt.reshape(bp * N_MEM, d)

    p_ssm, p_conv, p_k, p_v, p_mk, p_mv = [], [], [], [], [], []
    s_ssm, s_conv, s_k, s_v = [], [], [], []
    for i in range(depth):
        j = i // 2
        last = i == depth - 1
        xs = [_ffn(x, g_ffn1[i], w_ffn1_in, w_ffn1_out, i) for x in xs]

        mk = _proj(mem2d, g_mem[i], w_mem_kv, i, 0, D_MEM)
        mv = _proj(mem2d, g_mem[i], w_mem_kv, i, D_MEM, D_MEM)
        p_mk.append(mk.reshape(bp, N_MEM, MEM_HEADS, MEM_HEAD_DIM))
        p_mv.append(mv.reshape(bp, N_MEM, MEM_HEADS, MEM_HEAD_DIM))
        mk3, mv3 = mk.reshape(bp, N_MEM, D_MEM), mv.reshape(bp, N_MEM, D_MEM)

        new_xs = []
        for gi, (x, (b, L)) in enumerate(zip(xs, groups)):
            prompt = gi == 0
            g = g_mix[i]
            if i % 2 == 0:
                assert L >= SSD_CONV - 1
                dt0 = D_INNER + CONV_DIM
                z = _proj(x, g, w_in_ssd, j, 0, D_INNER)
                xbc = _proj(x, g, w_in_ssd, j, D_INNER, CONV_DIM)
                dt = _proj(x, g, w_in_ssd[j:j + 1, :, dt0:dt0 + H], 0, 0, H)
                qm = _proj(x, g, w_in_ssd[j:j + 1, :, dt0 + H:], 0, 0, D_MEM, out_dtype=BF16)
                xbc3 = xbc.reshape(b, L, CONV_DIM)
                if prompt:
                    conv0, h0, q, ng = None, None, SSD_CHUNK, SSD_GROUPS
                else:
                    conv0, h0, q, ng = state_conv[j], state_ssm[j].reshape(b, H * P, N), L, SSD_GROUPS
                y, hfin = _ssd(z.reshape(b, L, D_INNER), xbc3, dt.reshape(b, L, H), conv0, h0,
                               conv_w[j], conv_b[j], dt_bias[j], a_log[j], d_skip[j], g_ssd_norm[j],
                               q=q, ng=ng)
                (p_conv if prompt else s_conv).append(xbc3[:, L - (SSD_CONV - 1):, :])
                (p_ssm if prompt else s_ssm).append(hfin.reshape(b, H, P, N))
                a1, w_out = y.reshape(b * L, D_INNER), w_out_ssd
            else:
                qh = _proj(x, g, w_in_sb, j, 0, D_SB, out_dtype=BF16)
                kh = _proj(x, g, w_in_sb, j, D_SB, D_SB)
                vh = _proj(x, g, w_in_sb, j, 2 * D_SB, D_SB)
                qm = _proj(x, g, w_in_sb, j, 3 * D_SB, D_MEM, out_dtype=BF16)
                q3, k3, v3 = (a.reshape(b, L, D_SB) for a in (qh, kh, vh))
                if prompt:
                    o = _sb_prompt(q3, k3, v3, sb_bias[j])
                else:
                    o = _sb_sample(q3, k3, v3, sb_bias[j], cache_k, cache_v, j, page_table)
                (p_k if prompt else s_k).append(kh.reshape(b, L, SB_HEADS, SB_HEAD_DIM))
                (p_v if prompt else s_v).append(vh.reshape(b, L, SB_HEADS, SB_HEAD_DIM))
                a1, w_out = o.reshape(b * L, D_SB), w_out_sb
            qm3 = qm.reshape(b, L, D_MEM)
            o_mem = _memattn(qm3, mk3, mv3) if prompt else _memattn_cache(qm3, cache_mem_k, cache_mem_v, i)
            x = _outproj(x, a1, o_mem.reshape(b * L, D_MEM), w_out, j)
            x = _ffn(x, g_ffn2[i], w_ffn2_in, w_ffn2_out, i, g_final if last else None)
            new_xs.append(x)
        xs = new_xs

    return (xs[0].reshape(bp, lp, d), xs[1].reshape(bs, ls, d),
            jnp.stack(p_ssm), jnp.stack(p_conv), jnp.stack(p_k), jnp.stack(p_v),
            jnp.stack(p_mk), jnp.stack(p_mv), jnp.stack(s_ssm), jnp.stack(s_conv), jnp.stack(s_k), jnp.stack(s_v))
```

```python
import functools

import jax
import jax.numpy as jnp
from jax import lax
from jax.experimental import pallas as pl
from jax.experimental.pallas import tpu as pltpu

F32 = jnp.float32
BF16 = jnp.bfloat16

D_MODEL = 2048
N_MEM = 256
MEM_HEADS = 4
MEM_HEAD_DIM = D_MODEL // MEM_HEADS
D_MEM = MEM_HEADS * MEM_HEAD_DIM
D_INNER = 2 * D_MODEL
SSD_HEAD_DIM = 64
SSD_HEADS = D_INNER // SSD_HEAD_DIM
SSD_GROUPS = 8
SSD_HPG = SSD_HEADS // SSD_GROUPS
SSD_STATE = 128
SSD_CONV = 4
SSD_CHUNK = 256
GROUP_DIM = D_INNER // SSD_GROUPS
GROUP_CONV = GROUP_DIM + 2 * SSD_STATE
CONV_DIM = D_INNER + 2 * SSD_GROUPS * SSD_STATE
SB_HEAD_DIM = 128
SB_HEADS = D_MODEL // SB_HEAD_DIM
D_SB = SB_HEADS * SB_HEAD_DIM
PAGE_SIZE = 128
FFN_RES = 0.5
EPS = 1e-6

VMEM_LIMIT_BYTES = 56 * 1024 * 1024
SUBLANES = 8

_NT = (((1,), (1,)), ((), ()))
_TN = (((0,), (0,)), ((), ()))


def _params(*semantics):
    return pltpu.CompilerParams(dimension_semantics=semantics, vmem_limit_bytes=VMEM_LIMIT_BYTES)


FFN_VMEM_LIMIT_BYTES = 60 * 1024 * 1024


def _rms(x):
    return x * lax.rsqrt(jnp.mean(x * x, axis=-1, keepdims=True) + EPS)


def _silu(x):
    return x * jax.nn.sigmoid(x)


def _softplus(x):
    return jnp.maximum(x, 0.0) + jnp.log1p(jnp.exp(-jnp.abs(x)))


def _bdot(a, b, dims=None):
    a = a.astype(BF16)
    b = b.astype(BF16)
    if dims is None:
        return jnp.dot(a, b, preferred_element_type=F32)
    return lax.dot_general(a, b, dims, preferred_element_type=F32)


def _split3(x):
    x1 = x.astype(BF16)
    r1 = x - x1.astype(F32)
    x2 = r1.astype(BF16)
    return x1, x2, (r1 - x2.astype(F32)).astype(BF16)


def _dot01(a, b, *, exact):
    if exact == "a":
        return sum(jnp.dot(a.astype(BF16), p, preferred_element_type=F32) for p in _split3(b))
    return sum(jnp.dot(p, b.astype(BF16), preferred_element_type=F32) for p in _split3(a))


def _ffn_kernel(x_ref, g_ref, wg_ref, wu_ref, wo_ref, gf_ref, o_ref, h_ref, *, n_ff, final_norm):
    j = pl.program_id(1)

    @pl.when(j == 0)
    def _():
        h_ref[...] = (_rms(x_ref[...]) * g_ref[...]).astype(BF16)
        o_ref[...] = jnp.zeros_like(o_ref)

    h = h_ref[...]
    gate = _bdot(h, wg_ref[...])
    up = _bdot(h, wu_ref[...])
    o_ref[...] += _bdot(_silu(gate) * up, wo_ref[...])

    @pl.when(j == n_ff - 1)
    def _():
        y = x_ref[...] + FFN_RES * o_ref[...]
        if final_norm:
            y = _rms(y) * gf_ref[...]
        o_ref[...] = y


def _ffn(x, g, w_in, w_out, layer, g_final=None, *, tm=1024, tf=512):
    t, d = x.shape
    d_ff = w_out.shape[1]
    tm = min(tm, t)
    n_ff = d_ff // tf
    final_norm = g_final is not None
    gf = g_final if final_norm else g
    return pl.pallas_call(
        functools.partial(_ffn_kernel, n_ff=n_ff, final_norm=final_norm),
        grid=(t // tm, n_ff),
        in_specs=[
            pl.BlockSpec((tm, d), lambda i, j: (i, 0), pipeline_mode=pl.Buffered(1)),
            pl.BlockSpec((1, d), lambda i, j: (0, 0)),
            pl.BlockSpec((None, d, tf), lambda i, j: (layer, 0, j)),
            pl.BlockSpec((None, d, tf), lambda i, j: (layer, 0, n_ff + j)),
            pl.BlockSpec((None, tf, d), lambda i, j: (layer, j, 0)),
            pl.BlockSpec((1, d), lambda i, j: (0, 0)),
        ],
        out_specs=pl.BlockSpec((tm, d), lambda i, j: (i, 0), pipeline_mode=pl.Buffered(1)),
        out_shape=jax.ShapeDtypeStruct((t, d), F32),
        scratch_shapes=[pltpu.VMEM((tm, d), BF16)],
        compiler_params=pltpu.CompilerParams(dimension_semantics=("parallel", "arbitrary"),
                                             vmem_limit_bytes=FFN_VMEM_LIMIT_BYTES),
        name="ffn",
    )(x, g.reshape(1, d), w_in, w_in, w_out, gf.reshape(1, d))


def _proj_kernel(x_ref, g_ref, w_ref, o_ref, h_ref):
    @pl.when(pl.program_id(1) == 0)
    def _():
        h_ref[...] = (_rms(x_ref[...]) * g_ref[...]).astype(BF16)

    o_ref[...] = _bdot(h_ref[...], w_ref[...]).astype(o_ref.dtype)


def _proj(x, g, w, layer, col0, n, *, out_dtype=F32, tm=1024, tn=512):
    t, d = x.shape
    tm = min(tm, t)
    tn = min(tn, n)
    assert n % tn == 0 and col0 % tn == 0
    cb = col0 // tn
    return pl.pallas_call(
        _proj_kernel,
        grid=(t // tm, n // tn),
        in_specs=[
            pl.BlockSpec((tm, d), lambda i, j: (i, 0)),
            pl.BlockSpec((1, d), lambda i, j: (0, 0)),
            pl.BlockSpec((None, d, tn), lambda i, j: (layer, 0, cb + j)),
        ],
        out_specs=pl.BlockSpec((tm, tn), lambda i, j: (i, j)),
        out_shape=jax.ShapeDtypeStruct((t, n), out_dtype),
        scratch_shapes=[pltpu.VMEM((tm, d), BF16)],
        compiler_params=_params("parallel", "arbitrary"),
        name="proj",
    )(x, g.reshape(1, d), w)


def _outproj_kernel(x_ref, a1_ref, a2_ref, w1_ref, w2_ref, o_ref):
    o_ref[...] = x_ref[...] + _bdot(a1_ref[...], w1_ref[...]) + _bdot(a2_ref[...], w2_ref[...])


def _outproj(x, a1, a2, w, layer, *, tm=1024, tn=256):
    t, d = x.shape
    k1 = a1.shape[1]
    k2 = a2.shape[1]
    assert w.shape[1:] == (k1 + k2, d) and k1 % k2 == 0
    tm = min(tm, t)
    return pl.pallas_call(
        _outproj_kernel,
        grid=(t // tm, d // tn),
        in_specs=[
            pl.BlockSpec((tm, tn), lambda i, j: (i, j)),
            pl.BlockSpec((tm, k1), lambda i, j: (i, 0)),
            pl.BlockSpec((tm, k2), lambda i, j: (i, 0)),
            pl.BlockSpec((None, k1, tn), lambda i, j: (layer, 0, j)),
            pl.BlockSpec((None, k2, tn), lambda i, j: (layer, k1 // k2, j)),
        ],
        out_specs=pl.BlockSpec((tm, tn), lambda i, j: (i, j)),
        out_shape=jax.ShapeDtypeStruct((t, d), F32),
        compiler_params=_params("parallel", "arbitrary"),
        name="outproj",
    )(x, a1, a2, w, w)


def _mem_head(q, k, v):
    s = _bdot(q, k, _NT) * (MEM_HEAD_DIM ** -0.5)
    p = jnp.exp(s - jnp.max(s, axis=-1, keepdims=True))
    p = p / jnp.sum(p, axis=-1, keepdims=True)
    return _bdot(p, v)


def _memattn_kernel(q_ref, k_ref, v_ref, o_ref):
    for h in range(MEM_HEADS):
        sl = slice(h * MEM_HEAD_DIM, (h + 1) * MEM_HEAD_DIM)
        o_ref[:, sl] = _mem_head(q_ref[:, sl], k_ref[:, sl], v_ref[:, sl]).astype(o_ref.dtype)


def _memattn(q, mk, mv, *, tq=512):
    b, t, _ = q.shape
    tq = min(tq, t)
    kv_spec = pl.BlockSpec((None, N_MEM, D_MEM), lambda i, j: (i, 0, 0))
    return pl.pallas_call(
        _memattn_kernel,
        grid=(b, t // tq),
        in_specs=[pl.BlockSpec((None, tq, D_MEM), lambda i, j: (i, j, 0)), kv_spec, kv_spec],
        out_specs=pl.BlockSpec((None, tq, D_MEM), lambda i, j: (i, j, 0)),
        out_shape=jax.ShapeDtypeStruct((b, t, D_MEM), BF16),
        compiler_params=_params("parallel", "arbitrary"),
        name="memattn",
    )(q, mk, mv)


def _memattn_cache_kernel(q_ref, k_hbm, v_hbm, o_ref, kbuf, vbuf, sem, *, bb, layer):
    i = pl.program_id(0)
    slot = i % 2

    def head_copies(stp, slt):
        cps = []
        for bi in range(bb):
            for h in range(MEM_HEADS):
                for hbm, buf in ((k_hbm, kbuf), (v_hbm, vbuf)):
                    cps.append(pltpu.make_async_copy(
                        hbm.at[layer, stp * bb + bi, :, h, :], buf.at[slt, bi, h], sem.at[slt]))
        return cps

    @pl.when(i == 0)
    def _():
        for cp in head_copies(0, 0):
            cp.start()

    @pl.when(i + 1 < pl.num_programs(0))
    def _():
        for cp in head_copies(i + 1, 1 - slot):
            cp.start()

    for cp in head_copies(i, slot):
        cp.wait()
    for bi in range(bb):
        for h in range(MEM_HEADS):
            sl = slice(h * MEM_HEAD_DIM, (h + 1) * MEM_HEAD_DIM)
            o_ref[bi, :, sl] = _mem_head(q_ref[bi, :, sl], kbuf[slot, bi, h], vbuf[slot, bi, h]).astype(o_ref.dtype)


def _memattn_cache(q, cache_k, cache_v, layer, *, bb=2):
    b, t, _ = q.shape
    assert b % bb == 0
    stage = pltpu.VMEM((2, bb, MEM_HEADS, N_MEM, MEM_HEAD_DIM), F32)
    return pl.pallas_call(
        functools.partial(_memattn_cache_kernel, bb=bb, layer=layer),
        grid=(b // bb,),
        in_specs=[pl.BlockSpec((bb, t, D_MEM), lambda i: (i, 0, 0)),
                  pl.BlockSpec(memory_space=pl.ANY), pl.BlockSpec(memory_space=pl.ANY)],
        out_specs=pl.BlockSpec((bb, t, D_MEM), lambda i: (i, 0, 0)),
        out_shape=jax.ShapeDtypeStruct((b, t, D_MEM), BF16),
        scratch_shapes=[stage, stage, pltpu.SemaphoreType.DMA((2,))],
        compiler_params=_params("arbitrary"),
        name="memattn_cache",
    )(q, cache_k, cache_v)


def _ssd_kernel(z_ref, x_ref, b_ref, c_ref, dtc_ref, dtr_ref,
                cx0_ref, cb0_ref, cc0_ref, h0_ref,
                wx_ref, wb_ref, wc_ref, bx_ref, bb_ref, bc_ref,
                dbc_ref, dbr_ref, alc_ref, alr_ref, dsk_ref, gn_ref,
                y_ref, hfin_ref,
                xp_ref, bp_ref, cp_ref, h_ref, yd_ref, *, q, ng, n_chunks, zero_init):
    c = pl.program_id(2)
    R, P, N = SSD_HPG, SSD_HEAD_DIM, SSD_STATE
    pad = SUBLANES

    @pl.when(c == 0)
    def _():
        for ref, init in ((xp_ref, cx0_ref), (bp_ref, cb0_ref), (cp_ref, cc0_ref)):
            ref[0:pad, :] = jnp.zeros((pad, ref.shape[1]), F32)
            if not zero_init:
                ref[pad - (SSD_CONV - 1):pad, :] = init[...]
        if zero_init:
            h_ref[...] = jnp.zeros_like(h_ref)
        else:
            h_ref[...] = h0_ref[...]

    def conv(raw_ref, p_ref, w_ref, bias_ref):
        p_ref[pad:pad + q, :] = raw_ref[...]
        acc = bias_ref[...] + w_ref[SSD_CONV - 1:SSD_CONV, :] * p_ref[pad:pad + q, :]
        for k in range(SSD_CONV - 1):
            off = pad - (SSD_CONV - 1) + k
            acc = acc + w_ref[k:k + 1, :] * p_ref[off:off + q, :]
        p_ref[0:pad, :] = p_ref[q:q + pad, :]
        return _silu(acc)

    xc = conv(x_ref, xp_ref, wx_ref, bx_ref)
    bm = conv(b_ref, bp_ref, wb_ref, bb_ref)
    cm = conv(c_ref, cp_ref, wc_ref, bc_ref)

    dt_c = _softplus(dtc_ref[...] + dbc_ref[...])
    dt_r = _softplus(dtr_ref[...] + dbr_ref[...])
    da_c = dt_c * (-jnp.exp(alc_ref[...]))
    da_r = dt_r * (-jnp.exp(alr_ref[...]))
    ti = lax.broadcasted_iota(jnp.int32, (q, q), 0)
    si = lax.broadcasted_iota(jnp.int32, (q, q), 1)
    tril = si <= ti
    cs_c = _dot01(tril, da_c, exact="a")
    cs_r = _dot01(da_r, ti <= si, exact="b")

    hi = lax.broadcasted_iota(jnp.int32, (R, R * P), 0)
    ci = lax.broadcasted_iota(jnp.int32, (R, R * P), 1)
    expand = ((ci >= hi * P) & (ci < (hi + 1) * P)).astype(F32)

    for gi in range(ng):
        gs = slice(gi * R * P, (gi + 1) * R * P)
        ns = slice(gi * N, (gi + 1) * N)
        hs = slice(gi * R, (gi + 1) * R)
        x_g, b_g, c_g = xc[:, gs], bm[:, ns], cm[:, ns]
        csx = _dot01(cs_c[:, hs], expand, exact="b")
        xdt = x_g * _dot01(dt_c[:, hs], expand, exact="b")
        cb = _bdot(c_g, b_g, _NT)

        if q <= SUBLANES:
            trow = lax.broadcasted_iota(jnp.int32, (q, R * P), 0)
            yd = jnp.zeros((q, R * P), F32)
            for s in range(q):
                w = jnp.exp(jnp.where(trow >= s, csx - csx[s:s + 1, :], -jnp.inf))
                yd = yd + (cb[:, s:s + 1] * w) * xdt[s:s + 1, :]
        else:
            for r in range(R):
                col = cs_c[:, gi * R + r:gi * R + r + 1]
                row = cs_r[gi * R + r:gi * R + r + 1, :]
                m = cb * jnp.exp(jnp.where(tril, col - row, -jnp.inf))
                yd_ref[:, r * P:(r + 1) * P] = _bdot(m, xdt[:, r * P:(r + 1) * P])
            yd = yd_ref[...]

        h_old = h_ref[gs, :]
        y_off = _bdot(c_g, h_old, _NT) * jnp.exp(csx)
        x_end = xdt * jnp.exp(csx[q - 1:q, :] - csx)
        st = _bdot(x_end, b_g, _TN)
        for r in range(R):
            rs = slice(gi * R * P + r * P, gi * R * P + (r + 1) * P)
            dec = jnp.exp(cs_r[gi * R + r:gi * R + r + 1, q - 1:q])
            h_ref[rs, :] = h_ref[rs, :] * dec + st[r * P:(r + 1) * P, :]

        y = yd + y_off + dsk_ref[:, gs] * x_g
        y = y * _silu(z_ref[:, gs])
        y_ref[:, gs] = (_rms(y) * gn_ref[:, gs]).astype(y_ref.dtype)

    @pl.when(c == n_chunks - 1)
    def _():
        hfin_ref[...] = h_ref[...]


def _ssd(z, xbc, dt, conv0, h0, conv_w, conv_b, dt_bias, a_log, d_skip, g_norm, *, q, ng):
    b, L, _ = z.shape
    G, R, P, N = SSD_GROUPS, SSD_HPG, SSD_HEAD_DIM, SSD_STATE
    zero_init = h0 is None
    if zero_init:
        conv0 = jnp.zeros((b, SSD_CONV - 1, CONV_DIM), F32)
        h0 = jnp.zeros((b, SUBLANES, N), F32)
    ngrp = G // ng
    n_chunks = L // q
    gw, nw, hw = ng * R * P, ng * N, ng * R
    b_blk0 = D_INNER // nw
    c_blk0 = (D_INNER + G * N) // nw

    dt_c = dt.reshape(b, L, ngrp, hw).transpose(0, 2, 1, 3)
    dt_r = dt_c.transpose(0, 1, 3, 2)
    dbc = dt_bias.reshape(ngrp, 1, hw)
    dbr = dt_bias.reshape(ngrp, hw, 1)
    alc = a_log.reshape(ngrp, 1, hw)
    alr = a_log.reshape(ngrp, hw, 1)
    dsk = jnp.repeat(d_skip, P).reshape(1, D_INNER)
    gn = g_norm.reshape(1, D_INNER)
    cbias = conv_b.reshape(1, CONV_DIM)

    seq = lambda bi, g, c: (bi, c, g)
    kern = functools.partial(_ssd_kernel, q=q, ng=ng, n_chunks=n_chunks, zero_init=zero_init)
    h0_spec = (pl.BlockSpec((None, SUBLANES, N), lambda bi, g, c: (bi, 0, 0)) if zero_init
               else pl.BlockSpec((None, gw, N), lambda bi, g, c: (bi, g, 0)))
    y, hfin = pl.pallas_call(
        kern,
        grid=(b, ngrp, n_chunks),
        in_specs=[
            pl.BlockSpec((None, q, gw), seq),
            pl.BlockSpec((None, q, gw), seq),
            pl.BlockSpec((None, q, nw), lambda bi, g, c: (bi, c, b_blk0 + g)),
            pl.BlockSpec((None, q, nw), lambda bi, g, c: (bi, c, c_blk0 + g)),
            pl.BlockSpec((None, None, q, hw), lambda bi, g, c: (bi, g, c, 0)),
            pl.BlockSpec((None, None, hw, q), lambda bi, g, c: (bi, g, 0, c)),
            pl.BlockSpec((None, SSD_CONV - 1, gw), lambda bi, g, c: (bi, 0, g)),
            pl.BlockSpec((None, SSD_CONV - 1, nw), lambda bi, g, c: (bi, 0, b_blk0 + g)),
            pl.BlockSpec((None, SSD_CONV - 1, nw), lambda bi, g, c: (bi, 0, c_blk0 + g)),
            h0_spec,
            pl.BlockSpec((SSD_CONV, gw), lambda bi, g, c: (0, g)),
            pl.BlockSpec((SSD_CONV, nw), lambda bi, g, c: (0, b_blk0 + g)),
            pl.BlockSpec((SSD_CONV, nw), lambda bi, g, c: (0, c_blk0 + g)),
            pl.BlockSpec((1, gw), lambda bi, g, c: (0, g)),
            pl.BlockSpec((1, nw), lambda bi, g, c: (0, b_blk0 + g)),
            pl.BlockSpec((1, nw), lambda bi, g, c: (0, c_blk0 + g)),
            pl.BlockSpec((None, 1, hw), lambda bi, g, c: (g, 0, 0)),
            pl.BlockSpec((None, hw, 1), lambda bi, g, c: (g, 0, 0)),
            pl.BlockSpec((None, 1, hw), lambda bi, g, c: (g, 0, 0)),
            pl.BlockSpec((None, hw, 1), lambda bi, g, c: (g, 0, 0)),
            pl.BlockSpec((1, gw), lambda bi, g, c: (0, g)),
            pl.BlockSpec((1, gw), lambda bi, g, c: (0, g)),
        ],
        out_specs=[
            pl.BlockSpec((None, q, gw), seq),
            pl.BlockSpec((None, gw, N), lambda bi, g, c: (bi, g, 0)),
        ],
        out_shape=[
            jax.ShapeDtypeStruct((b, L, D_INNER), BF16),
            jax.ShapeDtypeStruct((b, SSD_HEADS * P, N), F32),
        ],
        scratch_shapes=[
            pltpu.VMEM((q + SUBLANES, gw), F32),
            pltpu.VMEM((q + SUBLANES, nw), F32),
            pltpu.VMEM((q + SUBLANES, nw), F32),
            pltpu.VMEM((gw, N), F32),
            pltpu.VMEM((q, R * P), F32),
        ],
        compiler_params=_params("parallel", "parallel", "arbitrary"),
        name="ssd",
    )(z, xbc, xbc, xbc, dt_c, dt_r, conv0, conv0, conv0, h0,
      conv_w, conv_w, conv_w, cbias, cbias, cbias, dbc, dbr, alc, alr, dsk, gn)
    return y, hfin


LOG2E = 1.4426950408889634
SB_LOGIT_SCALE = SB_HEAD_DIM ** -0.5 * LOG2E


def _sb_weights(z2, carry, mask):
    n = z2.shape[1]
    neg_abs = pltpu.bitcast(pltpu.bitcast(z2, jnp.uint32) | jnp.uint32(0x80000000), F32)
    drop = jnp.maximum(z2, 0.0) + jnp.log(1.0 + jnp.exp2(neg_abs)) * LOG2E
    if mask is not None:
        drop = jnp.where(mask, drop, 0.0)
    si = lax.broadcasted_iota(jnp.int32, (2 * n, n), 0)
    ji = lax.broadcasted_iota(jnp.int32, (2 * n, n), 1)
    later = ((si % n) > ji).astype(BF16)
    hi = drop.astype(BF16)
    lo = (drop - hi.astype(F32)).astype(BF16)
    after = jnp.dot(jnp.concatenate([hi, lo], axis=1), later, preferred_element_type=F32) + carry
    attn = jnp.exp2((z2 - drop) - after)
    if mask is not None:
        attn = jnp.where(mask, attn, 0.0)
    return attn, carry + jnp.sum(drop, axis=-1, keepdims=True)


def _sbp_kernel(bias_ref, q_ref, k_ref, v_ref, o_ref, *, tq, nh):
    qi = pl.program_id(2)
    dh = SB_HEAD_DIM
    hp = pl.program_id(1)
    qs = [q_ref[:, h * dh:(h + 1) * dh].astype(BF16) for h in range(nh)]
    bias = jnp.concatenate(
        [jnp.full((tq, 1), bias_ref[hp * nh + h] * LOG2E, F32) for h in range(nh)], axis=0)
    ti = lax.broadcasted_iota(jnp.int32, (nh * tq, tq), 0)
    ji = lax.broadcasted_iota(jnp.int32, (nh * tq, tq), 1)

    def blocks(start, state, mask):
        carry, accs = state
        heads = [slice(h * dh, (h + 1) * dh) for h in range(nh)]
        z = jnp.concatenate([_bdot(qs[h], k_ref[pl.ds(start, tq), heads[h]], _NT) for h in range(nh)],
                            axis=0) * SB_LOGIT_SCALE + bias
        attn, carry = _sb_weights(z, carry, mask)
        accs = tuple(accs[h] + _bdot(attn[h * tq:(h + 1) * tq, :], v_ref[pl.ds(start, tq), heads[h]])
                     for h in range(nh))
        return carry, accs

    init = (jnp.zeros((nh * tq, 1), F32), tuple(jnp.zeros((tq, dh), F32) for _ in range(nh)))
    block_at = lambda kb, st: blocks(pl.multiple_of(kb * tq, tq), st, None)
    state = blocks(pl.multiple_of(qi * tq, tq), init, ji < (ti % tq))
    state = lax.cond(qi % 2 == 1, lambda st: block_at(qi - 1, st), lambda st: st, state)
    top = qi - qi % 2
    _, accs = lax.fori_loop(
        0, qi // 2, lambda i, st: block_at(top - 2 - 2 * i, block_at(top - 1 - 2 * i, st)), state)
    for h in range(nh):
        o_ref[:, h * dh:(h + 1) * dh] = accs[h].astype(o_ref.dtype)


def _sb_prompt(q, k, v, bias, *, tq=256, nh=8):
    b, L, _ = q.shape
    w = nh * SB_HEAD_DIM
    kv_spec = pl.BlockSpec((None, L, w), lambda bi, h, i: (bi, 0, h), pipeline_mode=pl.Buffered(1))
    return pl.pallas_call(
        functools.partial(_sbp_kernel, tq=tq, nh=nh),
        grid=(b, SB_HEADS // nh, L // tq),
        in_specs=[
            pl.BlockSpec(memory_space=pltpu.SMEM),
            pl.BlockSpec((None, tq, w), lambda bi, h, i: (bi, i, h)),
            kv_spec, kv_spec,
        ],
        out_specs=pl.BlockSpec((None, tq, w), lambda bi, h, i: (bi, i, h)),
        out_shape=jax.ShapeDtypeStruct((b, L, D_SB), BF16),
        compiler_params=_params("parallel", "parallel", "arbitrary"),
        name="sb_prompt",
    )(bias, q, k, v)


def _sbs_kernel(pt_ref, bias_ref, q_ref, kn_ref, vn_ref, k_hbm, v_hbm, o_ref,
                kbuf, vbuf, sem, acc_ref, carry_ref, *, t, n_steps, pps, layer):
    bi, step = pl.program_id(0), pl.program_id(1)
    H, dh = SB_HEADS, SB_HEAD_DIM
    n_pages = n_steps * pps
    m = H * t
    slot = (bi * n_steps + step) % 2

    def page_copies(seq, stp, slt):
        cps = []
        for i in range(pps):
            page = pt_ref[seq, n_pages - 1 - (stp * pps + i)]
            for h in range(H):
                for hbm, buf in ((k_hbm, kbuf), (v_hbm, vbuf)):
                    cps.append(pltpu.make_async_copy(
                        hbm.at[layer, page, :, h, :], buf.at[slt, i, h], sem.at[slt]))
        return cps

    @pl.when((bi == 0) & (step == 0))
    def _():
        for cp in page_copies(0, 0, 0):
            cp.start()

    wrap = step == n_steps - 1
    nxt_seq = jnp.where(wrap, bi + 1, bi)
    nxt_step = jnp.where(wrap, 0, step + 1)

    @pl.when(nxt_seq < pl.num_programs(0))
    def _():
        for cp in page_copies(nxt_seq, nxt_step, 1 - slot):
            cp.start()

    qs = [q_ref[:, h * dh:(h + 1) * dh].astype(BF16) for h in range(H)]
    bias = bias_ref[...] * LOG2E

    def attend(k_of, v_of, carry, acc, mask):
        z = jnp.concatenate([_bdot(qs[h], k_of(h), _NT) for h in range(H)], axis=0) * SB_LOGIT_SCALE + bias
        attn, carry = _sb_weights(z, carry, mask)
        o = jnp.concatenate([_bdot(attn[h * t:(h + 1) * t, :], v_of(h)) for h in range(H)], axis=0)
        return carry, acc + o

    @pl.when(step == 0)
    def _():
        pad = jnp.zeros((PAGE_SIZE - t, dh), F32)
        new = lambda ref: (lambda h: jnp.concatenate([ref[:, h * dh:(h + 1) * dh], pad], axis=0))
        ri = lax.broadcasted_iota(jnp.int32, (m, PAGE_SIZE), 0)
        ki = lax.broadcasted_iota(jnp.int32, (m, PAGE_SIZE), 1)
        carry, acc = attend(new(kn_ref), new(vn_ref), jnp.zeros((m, 1), F32),
                            jnp.zeros((m, dh), F32), ki < (ri % t))
        carry_ref[...] = carry
        acc_ref[...] = acc

    for cp in page_copies(bi, step, slot):
        cp.wait()
    carry, acc = carry_ref[...], acc_ref[...]
    for i in range(pps):
        head = lambda view: (lambda h: view[slot, i, h])
        carry, acc = attend(head(kbuf), head(vbuf), carry, acc, None)
    carry_ref[...] = carry
    acc_ref[...] = acc

    @pl.when(step == n_steps - 1)
    def _():
        for h in range(H):
            o_ref[:, h * dh:(h + 1) * dh] = acc_ref[h * t:(h + 1) * t, :].astype(o_ref.dtype)


def _sb_sample(q, k, v, bias, cache_k, cache_v, layer, page_table, *, pps=8):
    b, t, _ = q.shape
    H, dh = SB_HEADS, SB_HEAD_DIM
    n_pages = page_table.shape[1]
    pps = min(pps, n_pages)
    assert n_pages % pps == 0
    n_steps = n_pages // pps
    m = H * t
    bias_rows = jnp.repeat(bias, t).reshape(m, 1)
    tok = lambda bi, s, pt: (bi, 0, 0)
    stage = pltpu.VMEM((2, pps, H, PAGE_SIZE, dh), F32)
    return pl.pallas_call(
        functools.partial(_sbs_kernel, t=t, n_steps=n_steps, pps=pps, layer=layer),
        grid_spec=pltpu.PrefetchScalarGridSpec(
            num_scalar_prefetch=1,
            grid=(b, n_steps),
            in_specs=[
                pl.BlockSpec((m, 1), lambda bi, s, pt: (0, 0)),
                pl.BlockSpec((None, t, D_SB), tok),
                pl.BlockSpec((None, t, D_SB), tok),
                pl.BlockSpec((None, t, D_SB), tok),
                pl.BlockSpec(memory_space=pl.ANY),
                pl.BlockSpec(memory_space=pl.ANY),
            ],
            out_specs=pl.BlockSpec((None, t, D_SB), tok),
            scratch_shapes=[
                stage, stage,
                pltpu.SemaphoreType.DMA((2,)),
                pltpu.VMEM((m, dh), F32),
                pltpu.VMEM((m, 1), F32),
            ],
        ),
        out_shape=jax.ShapeDtypeStruct((b, t, D_SB), BF16),
        compiler_params=_params("arbitrary", "arbitrary"),
        name="sb_sample",
    )(page_table, bias_rows, q, k, v, cache_k, cache_v)


def kernel(x_prompt, x_sample, mem_prompt, state_ssm, state_conv, cache_k, cache_v, cache_mem_k, cache_mem_v,
           page_table, g_ffn1, w_ffn1_in, w_ffn1_out, g_mix, g_mem, w_mem_kv, w_in_ssd, conv_w, conv_b,
           dt_bias, a_log, d_skip, g_ssd_norm, w_out_ssd, w_in_sb, sb_bias, w_out_sb, g_ffn2, w_ffn2_in,
           w_ffn2_out, g_final):
    bp, lp, d = x_prompt.shape
    bs, ls, _ = x_sample.shape
    depth = g_ffn1.shape[0]
    H, P, N = SSD_HEADS, SSD_HEAD_DIM, SSD_STATE
    groups = ((bp, lp), (bs, ls))
    xs = [x_prompt.reshape(bp * lp, d), x_sample.reshape(bs * ls, d)]
    mem2d = mem_prompt.reshape(bp * N_MEM, d)

    p_ssm, p_conv, p_k, p_v, p_mk, p_mv = [], [], [], [], [], []
    s_ssm, s_conv, s_k, s_v = [], [], [], []
    for i in range(depth):
        j = i // 2
        last = i == depth - 1
        xs = [_ffn(x, g_ffn1[i], w_ffn1_in, w_ffn1_out, i) for x in xs]

        mk = _proj(mem2d, g_mem[i], w_mem_kv, i, 0, D_MEM)
        mv = _proj(mem2d, g_mem[i], w_mem_kv, i, D_MEM, D_MEM)
        p_mk.append(mk.reshape(bp, N_MEM, MEM_HEADS, MEM_HEAD_DIM))
        p_mv.append(mv.reshape(bp, N_MEM, MEM_HEADS, MEM_HEAD_DIM))
        mk3, mv3 = mk.reshape(bp, N_MEM, D_MEM), mv.reshape(bp, N_MEM, D_MEM)

        new_xs = []
        for gi, (x, (b, L)) in enumerate(zip(xs, groups)):
            prompt = gi == 0
            g = g_mix[i]
            if i % 2 == 0:
                assert L >= SSD_CONV - 1
                dt0 = D_INNER + CONV_DIM
                z = _proj(x, g, w_in_ssd, j, 0, D_INNER)
                xbc = _proj(x, g, w_in_ssd, j, D_INNER, CONV_DIM)
                dt = _proj(x, g, w_in_ssd[j:j + 1, :, dt0:dt0 + H], 0, 0, H)
                qm = _proj(x, g, w_in_ssd[j:j + 1, :, dt0 + H:], 0, 0, D_MEM, out_dtype=BF16)
                xbc3 = xbc.reshape(b, L, CONV_DIM)
                if prompt:
                    conv0, h0, q, ng = None, None, SSD_CHUNK, SSD_GROUPS
                else:
                    conv0, h0, q, ng = state_conv[j], state_ssm[j].reshape(b, H * P, N), L, SSD_GROUPS
                y, hfin = _ssd(z.reshape(b, L, D_INNER), xbc3, dt.reshape(b, L, H), conv0, h0,
                               conv_w[j], conv_b[j], dt_bias[j], a_log[j], d_skip[j], g_ssd_norm[j],
                               q=q, ng=ng)
                (p_conv if prompt else s_conv).append(xbc3[:, L - (SSD_CONV - 1):, :])
                (p_ssm if prompt else s_ssm).append(hfin.reshape(b, H, P, N))
                a1, w_out = y.reshape(b * L, D_INNER), w_out_ssd
            else:
                qh = _proj(x, g, w_in_sb, j, 0, D_SB, out_dtype=BF16)
                kh = _proj(x, g, w_in_sb, j, D_SB, D_SB)
                vh = _proj(x, g, w_in_sb, j, 2 * D_SB, D_SB)
                qm = _proj(x, g, w_in_sb, j, 3 * D_SB, D_MEM, out_dtype=BF16)
                q3, k3, v3 = (a.reshape(b, L, D_SB) for a in (qh, kh, vh))
                if prompt:
                    o = _sb_prompt(q3, k3, v3, sb_bias[j])
                else:
                    o = _sb_sample(q3, k3, v3, sb_bias[j], cache_k, cache_v, j, page_table)
                (p_k if prompt else s_k).append(kh.reshape(b, L, SB_HEADS, SB_HEAD_DIM))
                (p_v if prompt else s_v).append(vh.reshape(b, L, SB_HEADS, SB_HEAD_DIM))
                a1, w_out = o.reshape(b * L, D_SB), w_out_sb
            qm3 = qm.reshape(b, L, D_MEM)
            o_mem = _memattn(qm3, mk3, mv3) if prompt else _memattn_cache(qm3, cache_mem_k, cache_mem_v, i)
            x = _outproj(x, a1, o_mem.reshape(b * L, D_MEM), w_out, j)
            x = _ffn(x, g_ffn2[i], w_ffn2_in, w_ffn2_out, i, g_final if last else None)
            new_xs.append(x)
        xs = new_xs

    return (xs[0].reshape(bp, lp, d), xs[1].reshape(bs, ls, d),
            jnp.stack(p_ssm), jnp.stack(p_conv), jnp.stack(p_k), jnp.stack(p_v),
            jnp.stack(p_mk), jnp.stack(p_mv), jnp.stack(s_ssm), jnp.stack(s_conv), jnp.stack(s_k), jnp.stack(s_v))
```
